```python
import jax
import jax.numpy as jnp
from jax import lax
import numpy as np

D_MODEL = 2048
BATCH = 2
SEQ = 4096
DEPTH = 4

GDN_HEAD_DIM = 128
GDN_HEADS = D_MODEL // GDN_HEAD_DIM
GDN_WIDTH = GDN_HEADS * GDN_HEAD_DIM
CONV_K = 4
CHUNK = 64
SWA_HEAD_DIM = 64
SWA_Q_HEADS = D_MODEL // SWA_HEAD_DIM
SWA_KV_HEADS = SWA_Q_HEADS // 8
SWA_Q_WIDTH = SWA_Q_HEADS * SWA_HEAD_DIM
SWA_KV_WIDTH = SWA_KV_HEADS * SWA_HEAD_DIM
WINDOW = 128
N_BRANCH = 2
D_FF = 4 * D_MODEL
PROJ_SIZES = (3 * GDN_WIDTH, GDN_WIDTH, GDN_HEADS, GDN_HEADS, SWA_Q_WIDTH, SWA_KV_WIDTH, SWA_KV_WIDTH, N_BRANCH * D_MODEL)
D_IN = 3 * GDN_WIDTH + GDN_WIDTH + 2 * GDN_HEADS + SWA_Q_WIDTH + 2 * SWA_KV_WIDTH + N_BRANCH * D_MODEL
NORM_EPS = 1e-6

kernel_name = "hybrid_gdn_swa_gated_parallel_trunk"


def rms_norm(x, gain):
    xf = x.astype(jnp.float32)
    y = xf * lax.rsqrt(jnp.mean(xf * xf, axis=-1, keepdims=True) + NORM_EPS)
    return (y * gain.astype(jnp.float32)).astype(x.dtype)


def l2_normalize(x):
    xf = x.astype(jnp.float32)
    return xf * lax.rsqrt(jnp.sum(xf * xf, axis=-1, keepdims=True) + NORM_EPS)


def causal_depthwise_conv(x, w):
    return lax.conv_general_dilated(
        x, w[:, None, :].astype(x.dtype), window_strides=(1,), padding=[(w.shape[0] - 1, 0)],
        dimension_numbers=("NWC", "WIO", "NWC"), feature_group_count=x.shape[-1])


def alibi_slopes(n_heads):
    return 2.0 ** (-8.0 * jnp.arange(1, n_heads + 1, dtype=jnp.float32) / n_heads)


def gated_delta_rule(q, k, v, g, beta):
    B, T, H, Dk = q.shape
    Dv = v.shape[-1]
    N = T // CHUNK
    f32 = jnp.float32

    def to_chunks(t):
        return t.astype(f32).reshape(B, N, CHUNK, H, -1).transpose(0, 3, 1, 2, 4)

    q = to_chunks(q) * (Dk ** -0.5)
    k = to_chunks(k)
    v = to_chunks(v)
    g = to_chunks(g[..., None])[..., 0]
    beta = to_chunks(beta[..., None])
    decay = jnp.cumsum(g, axis=-1)
    idx = jnp.arange(CHUNK)
    causal = idx[:, None] >= idx[None, :]
    strict = idx[:, None] > idx[None, :]
    gamma = jnp.exp(jnp.where(causal, decay[..., :, None] - decay[..., None, :], -jnp.inf))
    k_beta = k * beta
    a_low = jnp.where(strict, jnp.einsum("bhnid,bhnjd->bhnij", k_beta, k) * gamma, 0.0)
    rhs = jnp.concatenate([v * beta, k_beta * jnp.exp(decay)[..., None]], axis=-1)
    sol = lax.linalg.triangular_solve(a_low + jnp.eye(CHUNK, dtype=f32), rhs,
                                      left_side=True, lower=True, unit_diagonal=True)
    u, w = sol[..., :Dv], sol[..., Dv:]
    qk = jnp.einsum("bhnid,bhnjd->bhnij", q, k) * gamma
    q_dec = q * jnp.exp(decay)[..., None]
    k_dec = k * jnp.exp(decay[..., -1:] - decay)[..., None]
    chunk_decay = jnp.exp(decay[..., -1])

    def step(state, inp):
        qk_n, q_n, w_n, u_n, k_n, d_n = inp
        v_new = u_n - jnp.einsum("bhcd,bhde->bhce", w_n, state)
        o_n = jnp.einsum("bhcd,bhde->bhce", q_n, state) + jnp.einsum("bhij,bhje->bhie", qk_n, v_new)
        state = state * d_n[..., None, None] + jnp.einsum("bhcd,bhce->bhde", k_n, v_new)
        return state, o_n

    xs = tuple(jnp.moveaxis(t, 2, 0) for t in (qk, q_dec, w, u, k_dec, chunk_decay))
    _, o = lax.scan(step, jnp.zeros((B, H, Dk, Dv), f32), xs)
    return o.transpose(1, 0, 3, 2, 4).reshape(B, T, H, Dv)


def gdn_branch(qkv, z, b_logit, a_logit, conv_w, a_log, dt_bias, norm_g):
    B, T, _ = qkv.shape
    f32 = jnp.float32
    qkv_c = jax.nn.silu(causal_depthwise_conv(qkv, conv_w))
    q, k, v = jnp.split(qkv_c, 3, axis=-1)

    def heads(t):
        return t.reshape(B, T, GDN_HEADS, GDN_HEAD_DIM)

    q = l2_normalize(heads(q))
    k = l2_normalize(heads(k))
    beta = jax.nn.sigmoid(b_logit.astype(f32))
    g = -jnp.exp(a_log.astype(f32)) * jax.nn.softplus(a_logit.astype(f32) + dt_bias.astype(f32))
    o = gated_delta_rule(q, k, heads(v), g, beta)
    o = rms_norm(o, norm_g) * jax.nn.silu(heads(z).astype(f32))
    return o.reshape(B, T, GDN_WIDTH).astype(qkv.dtype)


def sliding_window_gqa(q, k, v, sinks):
    B, T, _ = q.shape
    NB = T // WINDOW
    G = SWA_Q_HEADS // SWA_KV_HEADS
    f32 = jnp.float32
    qb = q.reshape(B, NB, WINDOW, SWA_KV_HEADS, G, SWA_HEAD_DIM)

    def band(t):
        tb = t.reshape(B, NB, WINDOW, SWA_KV_HEADS, SWA_HEAD_DIM)
        prev = jnp.pad(tb, ((0, 0), (1, 0), (0, 0), (0, 0), (0, 0)))[:, :-1]
        return jnp.concatenate([prev, tb], axis=2)

    kb, vb = band(k), band(v)
    scores = jnp.einsum("bnqkgd,bnskd->bnkgqs", qb, kb).astype(f32) * (SWA_HEAD_DIM ** -0.5)
    qi = jnp.arange(WINDOW)[:, None]
    sj = jnp.arange(2 * WINDOW)[None, :]
    dist = qi + WINDOW - sj
    in_window = (dist >= 0) & (dist < WINDOW)
    after_start = (jnp.arange(NB)[:, None, None] > 0) | (sj >= WINDOW)[None]
    valid = in_window[None] & after_start
    slopes = alibi_slopes(SWA_Q_HEADS).reshape(SWA_KV_HEADS, G)
    bias = -slopes[:, :, None, None] * dist.astype(f32)
    scores = jnp.where(valid[None, :, None, None], scores + bias, -jnp.inf)
    sink = sinks.astype(f32).reshape(SWA_KV_HEADS, G)[:, :, None, None]
    m = jnp.maximum(scores.max(axis=-1, keepdims=True), sink)
    p = jnp.exp(scores - m)
    probs = (p / (p.sum(axis=-1, keepdims=True) + jnp.exp(sink - m))).astype(v.dtype)
    o = jnp.einsum("bnkgqs,bnskd->bnqkgd", probs, vb)
    return o.reshape(B, T, SWA_Q_WIDTH)


def setup_inputs(seed: int = 0) -> dict:
    key = jax.random.key(seed)
    ks = jax.random.split(key, 16)
    f32 = jnp.float32
    nrm = lambda k, shape, scale: jax.random.normal(k, shape, f32) * scale
    dt = jnp.exp(jax.random.uniform(ks[5], (DEPTH, GDN_HEADS), f32, np.log(1e-3), np.log(1e-1)))
    return {
        "x": jax.random.normal(ks[0], (BATCH, SEQ, D_MODEL), f32),
        "norm1_g": 1.0 + nrm(ks[1], (DEPTH, D_MODEL), 0.02),
        "w_in": nrm(ks[2], (DEPTH, D_MODEL, D_IN), D_MODEL ** -0.5),
        "conv_w": nrm(ks[3], (DEPTH, CONV_K, 3 * GDN_WIDTH), CONV_K ** -0.5),
        "a_log": jnp.log(jax.random.uniform(ks[4], (DEPTH, GDN_HEADS), f32, 1.0, 16.0)),
        "dt_bias": dt + jnp.log(-jnp.expm1(-dt)),
        "gdn_norm_g": 1.0 + nrm(ks[6], (DEPTH, GDN_HEAD_DIM), 0.02),
        "attn_sinks": nrm(ks[7], (DEPTH, SWA_Q_HEADS), 0.5),
        "w_branch_gdn": nrm(ks[8], (DEPTH, GDN_WIDTH, D_MODEL), GDN_WIDTH ** -0.5),
        "w_branch_swa": nrm(ks[9], (DEPTH, SWA_Q_WIDTH, D_MODEL), SWA_Q_WIDTH ** -0.5),
        "w_out": nrm(ks[10], (DEPTH, D_MODEL, D_MODEL), D_MODEL ** -0.5),
        "norm2_g": 1.0 + nrm(ks[11], (DEPTH, D_MODEL), 0.02),
        "w_ff_up": nrm(ks[12], (DEPTH, D_MODEL, D_FF), D_MODEL ** -0.5),
        "w_ff_down": nrm(ks[13], (DEPTH, D_FF, D_MODEL), D_FF ** -0.5),
        "final_norm_g": 1.0 + nrm(ks[14], (D_MODEL,), 0.02),
    }


def reference(x, norm1_g, w_in, conv_w, a_log, dt_bias, gdn_norm_g, attn_sinks, w_branch_gdn,
              w_branch_swa, w_out, norm2_g, w_ff_up, w_ff_down, final_norm_g):
    split_points = [int(s) for s in np.cumsum(PROJ_SIZES)[:-1]]
    for l in range(DEPTH):
        h = rms_norm(x, norm1_g[l])
        proj = h @ w_in[l]
        qkv_g, z, b_logit, a_logit, q_s, k_s, v_s, gate_logits = jnp.split(proj, split_points, axis=-1)
        y_gdn = gdn_branch(qkv_g, z, b_logit, a_logit, conv_w[l], a_log[l], dt_bias[l],
                           gdn_norm_g[l]) @ w_branch_gdn[l]
        y_swa = sliding_window_gqa(q_s, k_s, v_s, attn_sinks[l]) @ w_branch_swa[l]
        gates = jax.nn.sigmoid(gate_logits.astype(jnp.float32)).astype(x.dtype)
        g_gdn, g_swa = jnp.split(gates, N_BRANCH, axis=-1)
        x = x + (g_gdn * y_gdn + g_swa * y_swa) @ w_out[l]
        h = rms_norm(x, norm2_g[l])
        x = x + jnp.square(jax.nn.relu(h @ w_ff_up[l])) @ w_ff_down[l]
    return rms_norm(x, final_norm_g)
```

```python
import functools

import jax
import jax.numpy as jnp
from jax import lax
from jax.experimental import pallas as pl
from jax.experimental.pallas import tpu as pltpu

F32 = jnp.float32
BF16 = jnp.bfloat16

D_MODEL = 2048
GDN_HEAD_DIM = 128
GDN_HEADS = 16
GDN_WIDTH = GDN_HEADS * GDN_HEAD_DIM
CONV_K = 4
CHUNK = 64
GDN_HEAD_GROUP = 8
SWA_HEAD_DIM = 64
SWA_Q_HEADS = 32
SWA_KV_HEADS = 4
SWA_GROUP = SWA_Q_HEADS // SWA_KV_HEADS
SWA_Q_WIDTH = SWA_Q_HEADS * SWA_HEAD_DIM
SWA_KV_WIDTH = SWA_KV_HEADS * SWA_HEAD_DIM
WINDOW = 128
D_FF = 4 * D_MODEL
NORM_EPS = 1e-6

LANES = 128
SUBLANES = 8
VMEM_LIMIT_BYTES = 56 * 1024 * 1024

OFF_QKV = 0
OFF_Z = OFF_QKV + 3 * GDN_WIDTH
OFF_QS = OFF_Z + GDN_WIDTH
OFF_GATES = OFF_QS + SWA_Q_WIDTH
OFF_KS = OFF_GATES + 2 * D_MODEL
OFF_VS = OFF_KS + SWA_KV_WIDTH
OFF_BA = OFF_VS + SWA_KV_WIDTH
PROJ_WIDTH = OFF_BA + LANES
SRC_QKVZ = (0, 4 * GDN_WIDTH)
SRC_BA = (4 * GDN_WIDTH, 4 * GDN_WIDTH + 2 * GDN_HEADS)
SRC_QS = (SRC_BA[1], SRC_BA[1] + SWA_Q_WIDTH)
SRC_KS = (SRC_QS[1], SRC_QS[1] + SWA_KV_WIDTH)
SRC_VS = (SRC_KS[1], SRC_KS[1] + SWA_KV_WIDTH)
SRC_GATES = (SRC_VS[1], SRC_VS[1] + 2 * D_MODEL)


def _params(semantics):
    return pltpu.CompilerParams(dimension_semantics=semantics, vmem_limit_bytes=VMEM_LIMIT_BYTES)


def _rmsnorm_kernel(x_ref, g_ref, o_ref):
    x = x_ref[...]
    ms = jnp.mean(x * x, axis=-1, keepdims=True)
    o_ref[...] = (x * lax.rsqrt(ms + NORM_EPS) * g_ref[...]).astype(o_ref.dtype)


def _rmsnorm(x, gain, out_dtype, rows=512):
    n, d = x.shape
    return pl.pallas_call(
        _rmsnorm_kernel,
        grid=(n // rows,),
        in_specs=[pl.BlockSpec((rows, d), lambda i: (i, 0)),
                  pl.BlockSpec((1, d), lambda i: (0, 0))],
        out_specs=pl.BlockSpec((rows, d), lambda i: (i, 0)),
        out_shape=jax.ShapeDtypeStruct((n, d), out_dtype),
        compiler_params=_params(("parallel",)),
        name="rmsnorm",
    )(x, gain.reshape(1, d))


def _mm_kernel(*refs, nk, epilogue):
    if epilogue == "residual":
        a_ref, w_ref, r_ref, o_ref = refs[:4]
        rest = refs[4:]
    else:
        a_ref, w_ref, o_ref = refs[:3]
        r_ref = None
        rest = refs[3:]

    def finish(acc):
        if epilogue == "relu2":
            acc = jnp.square(jnp.maximum(acc, 0.0))
        elif epilogue == "residual":
            acc = r_ref[...] + acc
        o_ref[...] = acc.astype(o_ref.dtype)

    part = jnp.dot(a_ref[...], w_ref[...], preferred_element_type=F32)
    if nk == 1:
        finish(part)
        return
    acc_ref = rest[0]
    k = pl.program_id(2)

    @pl.when(k == 0)
    def _():
        acc_ref[...] = part

    @pl.when(jnp.logical_and(k > 0, k < nk - 1))
    def _():
        acc_ref[...] += part

    @pl.when(k == nk - 1)
    def _():
        finish(acc_ref[...] + part)


def _matmul(a, w, *, out_dtype, tm, tn, tk=None, epilogue="store", res=None, name="matmul"):
    m, kdim = a.shape
    n = w.shape[1]
    tk = kdim if tk is None else tk
    nk = kdim // tk
    grid = (m // tm, pl.cdiv(n, tn), nk)
    in_specs = [pl.BlockSpec((tm, tk), lambda i, j, k: (i, k)),
                pl.BlockSpec((tk, tn), lambda i, j, k: (k, j))]
    args = [a, w]
    if epilogue == "residual":
        in_specs.append(pl.BlockSpec((tm, tn), lambda i, j, k: (i, j)))
        args.append(res)
    scratch = [pltpu.VMEM((tm, tn), F32)] if nk > 1 else []
    return pl.pallas_call(
        functools.partial(_mm_kernel, nk=nk, epilogue=epilogue),
        grid=grid,
        in_specs=in_specs,
        out_specs=pl.BlockSpec((tm, tn), lambda i, j, k: (i, j)),
        out_shape=jax.ShapeDtypeStruct((m, n), out_dtype),
        scratch_shapes=scratch,
        compiler_params=_params(("parallel", "parallel", "arbitrary")),
        name=name,
    )(*args)


def _sigmoid(x):
    return 1.0 / (1.0 + jnp.exp(-x))


def _softplus(x):
    return jnp.maximum(x, 0.0) + jnp.log(1.0 + jnp.exp(-jnp.abs(x)))


def _cumsum_rows(x):
    rows = x.shape[0]
    row = lax.broadcasted_iota(jnp.int32, x.shape, 0)
    s = 1
    while s < rows:
        x = x + jnp.where(row >= s, pltpu.roll(x, s, axis=0), 0.0)
        s *= 2
    return x


def _shift_rows(x, halo, s):
    xr = pltpu.roll(x, s, axis=0)
    hr = pltpu.roll(halo, s, axis=0)
    row = lax.broadcasted_iota(jnp.int32, halo.shape, 0)
    first = jnp.where(row < s, hr, xr[:SUBLANES])
    return jnp.concatenate([first, xr[SUBLANES:]], axis=0)


def _conv_silu(x, halo, w):
    acc = x * w[CONV_K - 1:CONV_K]
    for s in range(1, CONV_K):
        acc = acc + _shift_rows(x, halo, s) * w[CONV_K - 1 - s:CONV_K - s]
    return acc * _sigmoid(acc)


def _dot(a, b):
    return jnp.dot(a.astype(BF16), b.astype(BF16), preferred_element_type=F32)


def _dot_nt(a, b):
    return lax.dot_general(a.astype(BF16), b.astype(BF16), (((1,), (1,)), ((), ())),
                           preferred_element_type=F32)


def _dot_tn(a, b):
    return lax.dot_general(a.astype(BF16), b.astype(BF16), (((0,), (0,)), ((), ())),
                           preferred_element_type=F32)


def _gdn_kernel(qkv_ref, z_ref, ba_ref, convw_ref, gp_ref, gnorm_ref, o_ref, state_ref, halo_ref):
    c = pl.program_id(1)

    @pl.when(c == 0)
    def _():
        state_ref[...] = jnp.zeros_like(state_ref)
        halo_ref[...] = jnp.zeros_like(halo_ref)

    ba = ba_ref[0]
    beta_all = _sigmoid(ba)
    g_all = -jnp.exp(gp_ref[0:1]) * _softplus(ba + gp_ref[1:2])
    decay_all = _cumsum_rows(g_all)
    decay_t = jnp.transpose(
        jnp.concatenate([decay_all, jnp.zeros((LANES - CHUNK, LANES), F32)], axis=0))

    ri = lax.broadcasted_iota(jnp.int32, (CHUNK, CHUNK), 0)
    ci = lax.broadcasted_iota(jnp.int32, (CHUNK, CHUNK), 1)
    causal = ri >= ci
    strict = ri > ci
    gnorm = gnorm_ref[...]
    q_scale = GDN_HEAD_DIM ** -0.5

    for g0 in range(0, GDN_HEADS, GDN_HEAD_GROUP):
        heads = range(g0, g0 + GDN_HEAD_GROUP)
        pre = []
        for h in heads:
            lo = h * GDN_HEAD_DIM
            sl_q = slice(lo, lo + GDN_HEAD_DIM)
            sl_k = slice(GDN_WIDTH + lo, GDN_WIDTH + lo + GDN_HEAD_DIM)
            sl_v = slice(2 * GDN_WIDTH + lo, 2 * GDN_WIDTH + lo + GDN_HEAD_DIM)
            qc = _conv_silu(qkv_ref[0, :, sl_q], halo_ref[:, sl_q], convw_ref[:, sl_q])
            kc = _conv_silu(qkv_ref[0, :, sl_k], halo_ref[:, sl_k], convw_ref[:, sl_k])
            vc = _conv_silu(qkv_ref[0, :, sl_v], halo_ref[:, sl_v], convw_ref[:, sl_v])
            qn = qc * (lax.rsqrt(jnp.sum(qc * qc, axis=-1, keepdims=True) + NORM_EPS) * q_scale)
            kn = kc * lax.rsqrt(jnp.sum(kc * kc, axis=-1, keepdims=True) + NORM_EPS)
            beta = beta_all[:, h:h + 1]
            dcol = decay_all[:, GDN_HEADS + h:GDN_HEADS + h + 1]
            drow = decay_t[GDN_HEADS + h:GDN_HEADS + h + 1, :CHUNK]
            dlast = dcol[CHUNK - 1:CHUNK]
            gamma = jnp.exp(jnp.where(causal, dcol - drow, -jnp.inf))
            edec = jnp.exp(dcol)
            kb = kn * beta
            pre.append(dict(
                sl=sl_q, qn=qn, kn=kn, kb=kb, gamma=gamma, q_dec=qn * edec,
                k_dec=kn * jnp.exp(dlast - dcol), sdec=jnp.exp(dlast),
                rhs=jnp.concatenate([vc * beta, kb * edec], axis=1)))

        r = [_dot_nt(jnp.concatenate([p["kb"], p["qn"]], axis=0), p["kn"]) for p in pre]
        a_low = [jnp.where(strict, ri_[:CHUNK] * p["gamma"], 0.0) for ri_, p in zip(r, pre)]
        qk = [ri_[CHUNK:] * p["gamma"] for ri_, p in zip(r, pre)]

        nmat = [-a for a in a_low]
        xpow = [_dot(a, a) for a in a_low]
        p2 = 2
        while 2 * p2 < CHUNK:
            prod = [_dot(jnp.concatenate([nm, xp], axis=0), xp) for nm, xp in zip(nmat, xpow)]
            nmat = [nm + xp + pr[:CHUNK] for nm, xp, pr in zip(nmat, xpow, prod)]
            xpow = [pr[CHUNK:] for pr in prod]
            p2 *= 2
        prod = [_dot(nm, xp) for nm, xp in zip(nmat, xpow)]
        nmat = [nm + xp + pr for nm, xp, pr in zip(nmat, xpow, prod)]

        uw = [p["rhs"] + _dot(nm, p["rhs"]) for nm, p in zip(nmat, pre)]
        states = [state_ref[h] for h in heads]
        ws_qs = [_dot(jnp.concatenate([uwi[:, GDN_HEAD_DIM:], p["q_dec"]], axis=0), st)
                 for uwi, p, st in zip(uw, pre, states)]
        v_new = [uwi[:, :GDN_HEAD_DIM] - wq[:CHUNK] for uwi, wq in zip(uw, ws_qs)]
        o_att = [_dot(qki, vn) for qki, vn in zip(qk, v_new)]
        kv_upd = [_dot_tn(p["k_dec"], vn) for p, vn in zip(pre, v_new)]
        for h, p, st, wq, oa, kvu in zip(heads, pre, states, ws_qs, o_att, kv_upd):
            state_ref[h] = st * p["sdec"] + kvu
            o = wq[CHUNK:] + oa
            on = o * lax.rsqrt(jnp.mean(o * o, axis=-1, keepdims=True) + NORM_EPS) * gnorm
            zc = z_ref[0, :, p["sl"]]
            o_ref[0, :, p["sl"]] = (on * (zc * _sigmoid(zc))).astype(o_ref.dtype)

    halo_ref[...] = qkv_ref[0, CHUNK - SUBLANES:CHUNK, :]


def _gdn(proj, conv_w, a_log, dt_bias, gnorm):
    b, t, _ = proj.shape
    gp = jnp.zeros((2, LANES), F32)
    gp = gp.at[0, GDN_HEADS:2 * GDN_HEADS].set(a_log).at[1, GDN_HEADS:2 * GDN_HEADS].set(dt_bias)
    qkv_w = 3 * GDN_WIDTH
    return pl.pallas_call(
        _gdn_kernel,
        grid=(b, t // CHUNK),
        in_specs=[
            pl.BlockSpec((1, CHUNK, qkv_w), lambda i, c: (i, c, OFF_QKV // qkv_w)),
            pl.BlockSpec((1, CHUNK, GDN_WIDTH), lambda i, c: (i, c, OFF_Z // GDN_WIDTH)),
            pl.BlockSpec((1, CHUNK, LANES), lambda i, c: (i, c, OFF_BA // LANES)),
            pl.BlockSpec((CONV_K, qkv_w), lambda i, c: (0, 0)),
            pl.BlockSpec((2, LANES), lambda i, c: (0, 0)),
            pl.BlockSpec((1, GDN_HEAD_DIM), lambda i, c: (0, 0)),
        ],
        out_specs=pl.BlockSpec((1, CHUNK, GDN_WIDTH), lambda i, c: (i, c, 0)),
        out_shape=jax.ShapeDtypeStruct((b, t, GDN_WIDTH), BF16),
        scratch_shapes=[pltpu.VMEM((GDN_HEADS, GDN_HEAD_DIM, GDN_HEAD_DIM), F32),
                        pltpu.VMEM((SUBLANES, qkv_w), F32)],
        compiler_params=_params(("parallel", "arbitrary")),
        name="gdn",
    )(proj, proj, proj, conv_w, gp, gnorm.reshape(1, GDN_HEAD_DIM))


def _swa_kernel(sink_ref, q_ref, kp_ref, kc_ref, vp_ref, vc_ref, o_ref):
    nb = pl.program_id(1)
    qi = lax.broadcasted_iota(jnp.int32, (WINDOW, 2 * WINDOW), 0)
    sj = lax.broadcasted_iota(jnp.int32, (WINDOW, 2 * WINDOW), 1)
    dist = qi + WINDOW - sj
    valid = (dist >= 0) & (dist < WINDOW) & ((nb > 0) | (sj >= WINDOW))
    dist_f = dist.astype(F32)
    scale = SWA_HEAD_DIM ** -0.5
    zeros_kv = jnp.zeros((2 * WINDOW, SWA_HEAD_DIM), BF16)

    for kv in range(SWA_KV_HEADS):
        sl = slice(kv * SWA_HEAD_DIM, (kv + 1) * SWA_HEAD_DIM)
        kband = jnp.concatenate([kp_ref[0, :, sl], kc_ref[0, :, sl]], axis=0).astype(BF16)
        vband = jnp.concatenate([vp_ref[0, :, sl], vc_ref[0, :, sl]], axis=0).astype(BF16)
        k_lo = jnp.concatenate([kband, zeros_kv], axis=1)
        k_hi = jnp.concatenate([zeros_kv, kband], axis=1)
        v_lo = jnp.concatenate([vband, zeros_kv], axis=1)
        v_hi = jnp.concatenate([zeros_kv, vband], axis=1)
        for pair in range(SWA_GROUP // 2):
            head0 = kv * SWA_GROUP + 2 * pair
            col = head0 * SWA_HEAD_DIM
            q_pair = q_ref[0, :, col:col + 2 * SWA_HEAD_DIM].astype(BF16)
            out = jnp.zeros((WINDOW, 2 * SWA_HEAD_DIM), F32)
            for half, (k_sel, v_sel) in enumerate(((k_lo, v_lo), (k_hi, v_hi))):
                head = head0 + half
                slope = 2.0 ** (-8.0 * (head + 1) / SWA_Q_HEADS)
                s = lax.dot_general(q_pair, k_sel, (((1,), (1,)), ((), ())),
                                    preferred_element_type=F32)
                s = jnp.where(valid, s * scale - slope * dist_f, -jnp.inf)
                sink = sink_ref[0, head]
                m = jnp.maximum(jnp.max(s, axis=-1, keepdims=True), sink)
                p = jnp.exp(s - m)
                denom = jnp.sum(p, axis=-1, keepdims=True) + jnp.exp(sink - m)
                probs = (p / denom).astype(BF16)
                out = out + jnp.dot(probs, v_sel, preferred_element_type=F32)
            o_ref[0, :, col:col + 2 * SWA_HEAD_DIM] = out.astype(o_ref.dtype)


def _swa(proj, sinks):
    b, t, _ = proj.shape
    kb = OFF_KS // SWA_KV_WIDTH
    vb = OFF_VS // SWA_KV_WIDTH
    prev = lambda i, n: jnp.maximum(n - 1, 0)
    return pl.pallas_call(
        _swa_kernel,
        grid=(b, t // WINDOW),
        in_specs=[
            pl.BlockSpec(memory_space=pltpu.SMEM),
            pl.BlockSpec((1, WINDOW, SWA_Q_WIDTH), lambda i, n: (i, n, OFF_QS // SWA_Q_WIDTH)),
            pl.BlockSpec((1, WINDOW, SWA_KV_WIDTH), lambda i, n: (i, prev(i, n), kb)),
            pl.BlockSpec((1, WINDOW, SWA_KV_WIDTH), lambda i, n: (i, n, kb)),
            pl.BlockSpec((1, WINDOW, SWA_KV_WIDTH), lambda i, n: (i, prev(i, n), vb)),
            pl.BlockSpec((1, WINDOW, SWA_KV_WIDTH), lambda i, n: (i, n, vb)),
        ],
        out_specs=pl.BlockSpec((1, WINDOW, SWA_Q_WIDTH), lambda i, n: (i, n, 0)),
        out_shape=jax.ShapeDtypeStruct((b, t, SWA_Q_WIDTH), BF16),
        compiler_params=_params(("parallel", "parallel")),
        name="swa",
    )(sinks.reshape(1, SWA_Q_HEADS), proj, proj, proj, proj, proj)


def _merge_kernel(og_ref, os_ref, wg_ref, ws_ref, gg_ref, gs_ref, o_ref):
    yg = jnp.dot(og_ref[...], wg_ref[...], preferred_element_type=F32)
    ys = jnp.dot(os_ref[...], ws_ref[...], preferred_element_type=F32)
    o_ref[...] = (_sigmoid(gg_ref[...]) * yg + _sigmoid(gs_ref[...]) * ys).astype(o_ref.dtype)


def _merge(o_gdn, o_swa, w_gdn, w_swa, proj2d, tm=1024, tn=512):
    m = o_gdn.shape[0]
    gate_blk = OFF_GATES // tn
    return pl.pallas_call(
        _merge_kernel,
        grid=(m // tm, D_MODEL // tn),
        in_specs=[
            pl.BlockSpec((tm, GDN_WIDTH), lambda i, j: (i, 0)),
            pl.BlockSpec((tm, SWA_Q_WIDTH), lambda i, j: (i, 0)),
            pl.BlockSpec((GDN_WIDTH, tn), lambda i, j: (0, j)),
            pl.BlockSpec((SWA_Q_WIDTH, tn), lambda i, j: (0, j)),
            pl.BlockSpec((tm, tn), lambda i, j: (i, gate_blk + j)),
            pl.BlockSpec((tm, tn), lambda i, j: (i, gate_blk + D_MODEL // tn + j)),
        ],
        out_specs=pl.BlockSpec((tm, tn), lambda i, j: (i, j)),
        out_shape=jax.ShapeDtypeStruct((m, D_MODEL), BF16),
        compiler_params=_params(("parallel", "parallel")),
        name="merge",
    )(o_gdn, o_swa, w_gdn, w_swa, proj2d, proj2d)


def _pack_w_in(w_in):
    seg = lambda r: w_in[:, :, r[0]:r[1]]
    pad = jnp.zeros(w_in.shape[:2] + (LANES - 2 * GDN_HEADS,), w_in.dtype)
    parts = [seg(SRC_QKVZ), seg(SRC_QS), seg(SRC_GATES), seg(SRC_KS), seg(SRC_VS), seg(SRC_BA), pad]
    return jnp.concatenate(parts, axis=-1).astype(BF16)


def kernel(x, norm1_g, w_in, conv_w, a_log, dt_bias, gdn_norm_g, attn_sinks, w_branch_gdn,
           w_branch_swa, w_out, norm2_g, w_ff_up, w_ff_down, final_norm_g):
    b, t, d = x.shape
    depth = w_in.shape[0]
    n = b * t
    w_in_p = _pack_w_in(w_in)
    w_gdn = w_branch_gdn.astype(BF16)
    w_swa = w_branch_swa.astype(BF16)
    w_o = w_out.astype(BF16)
    w_up = w_ff_up.astype(BF16)
    w_down = w_ff_down.astype(BF16)

    xf = x.reshape(n, d)
    for l in range(depth):
        h = _rmsnorm(xf, norm1_g[l], BF16)
        proj = _matmul(h, w_in_p[l], out_dtype=F32, tm=1024, tn=1152, name="in_proj")
        proj3 = proj.reshape(b, t, PROJ_WIDTH)
        o_gdn = _gdn(proj3, conv_w[l], a_log[l], dt_bias[l], gdn_norm_g[l]).reshape(n, GDN_WIDTH)
        o_swa = _swa(proj3, attn_sinks[l]).reshape(n, SWA_Q_WIDTH)
        merged = _merge(o_gdn, o_swa, w_gdn[l], w_swa[l], proj)
        xf = _matmul(merged, w_o[l], out_dtype=F32, tm=1024, tn=1024, epilogue="residual", res=xf,
                     name="out_proj")
        h2 = _rmsnorm(xf, norm2_g[l], BF16)
        act = _matmul(h2, w_up[l], out_dtype=BF16, tm=1024, tn=1024, epilogue="relu2", name="ff_up")
        xf = _matmul(act, w_down[l], out_dtype=F32, tm=1024, tn=1024, tk=2048, epilogue="residual",
                     res=xf, name="ff_down")
    return _rmsnorm(xf, final_norm_g, F32).reshape(b, t, d)
```

```python
import functools

import jax
import jax.numpy as jnp
from jax import lax
from jax.experimental import pallas as pl
from jax.experimental.pallas import tpu as pltpu

F32 = jnp.float32
BF16 = jnp.bfloat16

D_MODEL = 2048
GDN_HEAD_DIM = 128
GDN_HEADS = 16
GDN_WIDTH = GDN_HEADS * GDN_HEAD_DIM
CONV_K = 4
CHUNK = 64
GDN_HEAD_GROUP = 8
SWA_HEAD_DIM = 64
SWA_Q_HEADS = 32
SWA_KV_HEADS = 4
SWA_GROUP = SWA_Q_HEADS // SWA_KV_HEADS
SWA_Q_WIDTH = SWA_Q_HEADS * SWA_HEAD_DIM
SWA_KV_WIDTH = SWA_KV_HEADS * SWA_HEAD_DIM
WINDOW = 128
D_FF = 4 * D_MODEL
NORM_EPS = 1e-6

LANES = 128
SUBLANES = 8
VMEM_LIMIT_BYTES = 56 * 1024 * 1024

OFF_QKV = 0
OFF_Z = OFF_QKV + 3 * GDN_WIDTH
OFF_QS = OFF_Z + GDN_WIDTH
OFF_GATES = OFF_QS + SWA_Q_WIDTH
OFF_KS = OFF_GATES + 2 * D_MODEL
OFF_VS = OFF_KS + SWA_KV_WIDTH
OFF_BA = OFF_VS + SWA_KV_WIDTH
PROJ_WIDTH = OFF_BA + LANES
SRC_QKVZ = (0, 4 * GDN_WIDTH)
SRC_BA = (4 * GDN_WIDTH, 4 * GDN_WIDTH + 2 * GDN_HEADS)
SRC_QS = (SRC_BA[1], SRC_BA[1] + SWA_Q_WIDTH)
SRC_KS = (SRC_QS[1], SRC_QS[1] + SWA_KV_WIDTH)
SRC_VS = (SRC_KS[1], SRC_KS[1] + SWA_KV_WIDTH)
SRC_GATES = (SRC_VS[1], SRC_VS[1] + 2 * D_MODEL)


def _params(semantics):
    return pltpu.CompilerParams(dimension_semantics=semantics, vmem_limit_bytes=VMEM_LIMIT_BYTES)


def _rmsnorm_kernel(x_ref, g_ref, o_ref):
    x = x_ref[...]
    ms = jnp.mean(x * x, axis=-1, keepdims=True)
    o_ref[...] = (x * lax.rsqrt(ms + NORM_EPS) * g_ref[...]).astype(o_ref.dtype)


def _rmsnorm(x, gain, out_dtype, rows=512):
    n, d = x.shape
    return pl.pallas_call(
        _rmsnorm_kernel,
        grid=(n // rows,),
        in_specs=[pl.BlockSpec((rows, d), lambda i: (i, 0)),
                  pl.BlockSpec((1, d), lambda i: (0, 0))],
        out_specs=pl.BlockSpec((rows, d), lambda i: (i, 0)),
        out_shape=jax.ShapeDtypeStruct((n, d), out_dtype),
        compiler_params=_params(("parallel",)),
        name="rmsnorm",
    )(x, gain.reshape(1, d))


def _mm_kernel(*refs, nk, epilogue):
    if epilogue == "residual":
        a_ref, w_ref, r_ref, o_ref = refs[:4]
        rest = refs[4:]
    else:
        a_ref, w_ref, o_ref = refs[:3]
        r_ref = None
        rest = refs[3:]

    def finish(acc):
        if epilogue == "relu2":
            acc = jnp.square(jnp.maximum(acc, 0.0))
        elif epilogue == "residual":
            acc = r_ref[...] + acc
        o_ref[...] = acc.astype(o_ref.dtype)

    part = jnp.dot(a_ref[...], w_ref[...], preferred_element_type=F32)
    if nk == 1:
        finish(part)
        return
    acc_ref = rest[0]
    k = pl.program_id(2)

    @pl.when(k == 0)
    def _():
        acc_ref[...] = part

    @pl.when(jnp.logical_and(k > 0, k < nk - 1))
    def _():
        acc_ref[...] += part

    @pl.when(k == nk - 1)
    def _():
        finish(acc_ref[...] + part)


def _matmul(a, w, *, out_dtype, tm, tn, tk=None, epilogue="store", res=None, name="matmul"):
    m, kdim = a.shape
    n = w.shape[1]
    tk = kdim if tk is None else tk
    nk = kdim // tk
    grid = (m // tm, pl.cdiv(n, tn), nk)
    in_specs = [pl.BlockSpec((tm, tk), lambda i, j, k: (i, k)),
                pl.BlockSpec((tk, tn), lambda i, j, k: (k, j))]
    args = [a, w]
    if epilogue == "residual":
        in_specs.append(pl.BlockSpec((tm, tn), lambda i, j, k: (i, j)))
        args.append(res)
    scratch = [pltpu.VMEM((tm, tn), F32)] if nk > 1 else []
    return pl.pallas_call(
        functools.partial(_mm_kernel, nk=nk, epilogue=epilogue),
        grid=grid,
        in_specs=in_specs,
        out_specs=pl.BlockSpec((tm, tn), lambda i, j, k: (i, j)),
        out_shape=jax.ShapeDtypeStruct((m, n), out_dtype),
        scratch_shapes=scratch,
        compiler_params=_params(("parallel", "parallel", "arbitrary")),
        name=name,
    )(*args)


def _sigmoid(x):
    return 1.0 / (1.0 + jnp.exp(-x))


def _softplus(x):
    return jnp.maximum(x, 0.0) + jnp.log(1.0 + jnp.exp(-jnp.abs(x)))


def _cumsum_rows(x):
    rows = x.shape[0]
    row = lax.broadcasted_iota(jnp.int32, x.shape, 0)
    s = 1
    while s < rows:
        x = x + jnp.where(row >= s, pltpu.roll(x, s, axis=0), 0.0)
        s *= 2
    return x


def _shift_rows(x, halo, s):
    xr = pltpu.roll(x, s, axis=0)
    hr = pltpu.roll(halo, s, axis=0)
    row = lax.broadcasted_iota(jnp.int32, halo.shape, 0)
    first = jnp.where(row < s, hr, xr[:SUBLANES])
    return jnp.concatenate([first, xr[SUBLANES:]], axis=0)


def _conv_silu(x, halo, w):
    acc = x * w[CONV_K - 1:CONV_K]
    for s in range(1, CONV_K):
        acc = acc + _shift_rows(x, halo, s) * w[CONV_K - 1 - s:CONV_K - s]
    return acc * _sigmoid(acc)


def _dot(a, b):
    return jnp.dot(a.astype(BF16), b.astype(BF16), preferred_element_type=F32)


def _dot_nt(a, b):
    return lax.dot_general(a.astype(BF16), b.astype(BF16), (((1,), (1,)), ((), ())),
                           preferred_element_type=F32)


def _dot_tn(a, b):
    return lax.dot_general(a.astype(BF16), b.astype(BF16), (((0,), (0,)), ((), ())),
                           preferred_element_type=F32)


def _split_bf16(x):
    hi = x.astype(BF16)
    return hi, (x - hi.astype(F32)).astype(BF16)


def _block_diag_pair(xp, lane_lo):
    zero = jnp.zeros_like(xp)
    return jnp.concatenate([jnp.where(lane_lo, xp, zero), jnp.where(lane_lo, zero, xp)], axis=0)


def _pair_dot3(l, xp, lane_lo):
    l_hi, l_lo = _split_bf16(l)
    x_hi, x_lo = _split_bf16(xp)
    bd_hi = _block_diag_pair(x_hi, lane_lo)
    bd_lo = _block_diag_pair(x_lo, lane_lo)
    lhs = jnp.concatenate([l_hi, l_lo, l_hi], axis=1)
    rhs = jnp.concatenate([bd_hi, bd_hi, bd_lo], axis=0)
    return jnp.dot(lhs, rhs, preferred_element_type=F32)


def _gdn_kernel(qkv_ref, z_ref, ba_ref, convw_ref, gp_ref, gnorm_ref, o_ref, state_ref, halo_ref):
    c = pl.program_id(1)

    @pl.when(c == 0)
    def _():
        state_ref[...] = jnp.zeros_like(state_ref)
        halo_ref[...] = jnp.zeros_like(halo_ref)

    ba = ba_ref[0]
    beta_all = _sigmoid(ba)
    g_all = -jnp.exp(gp_ref[0:1]) * _softplus(ba + gp_ref[1:2])
    decay_all = _cumsum_rows(g_all)
    decay_t2 = jnp.transpose(jnp.concatenate([decay_all, decay_all], axis=0))

    ri = lax.broadcasted_iota(jnp.int32, (CHUNK, 2 * CHUNK), 0)
    ci = lax.broadcasted_iota(jnp.int32, (CHUNK, 2 * CHUNK), 1)
    lane_lo = ci < CHUNK
    cj = jnp.where(lane_lo, ci, ci - CHUNK)
    causal = ri >= cj
    strict = ri > cj
    lane_lo_row = lane_lo[0:1]
    gnorm = gnorm_ref[...]
    q_scale = GDN_HEAD_DIM ** -0.5
    zeros_hd = jnp.zeros((CHUNK, GDN_HEAD_DIM), F32)
    zeros_rhs = jnp.zeros((CHUNK, 2 * GDN_HEAD_DIM), F32)

    def head_inputs(h):
        lo = h * GDN_HEAD_DIM
        sl_q = slice(lo, lo + GDN_HEAD_DIM)
        sl_k = slice(GDN_WIDTH + lo, GDN_WIDTH + lo + GDN_HEAD_DIM)
        sl_v = slice(2 * GDN_WIDTH + lo, 2 * GDN_WIDTH + lo + GDN_HEAD_DIM)
        qc = _conv_silu(qkv_ref[0, :, sl_q], halo_ref[:, sl_q], convw_ref[:, sl_q])
        kc = _conv_silu(qkv_ref[0, :, sl_k], halo_ref[:, sl_k], convw_ref[:, sl_k])
        vc = _conv_silu(qkv_ref[0, :, sl_v], halo_ref[:, sl_v], convw_ref[:, sl_v])
        qn = qc * (lax.rsqrt(jnp.sum(qc * qc, axis=-1, keepdims=True) + NORM_EPS) * q_scale)
        kn = kc * lax.rsqrt(jnp.sum(kc * kc, axis=-1, keepdims=True) + NORM_EPS)
        beta = beta_all[:, h:h + 1]
        dcol = decay_all[:, GDN_HEADS + h:GDN_HEADS + h + 1]
        dlast = dcol[CHUNK - 1:CHUNK]
        edec = jnp.exp(dcol)
        kb = kn * beta
        return dict(sl=sl_q, qn=qn, kn=kn, kb=kb, dcol=dcol, q_dec=qn * edec,
                    k_dec=kn * jnp.exp(dlast - dcol), sdec=jnp.exp(dlast),
                    rhs=jnp.concatenate([vc * beta, kb * edec], axis=1))

    for g0 in range(0, GDN_HEADS, GDN_HEAD_GROUP):
        pairs = [(h, h + 1) for h in range(g0, g0 + GDN_HEAD_GROUP, 2)]
        pre = [(head_inputs(h0), head_inputs(h1)) for h0, h1 in pairs]

        gamma, r = [], []
        for (h0, h1), (p0, p1) in zip(pairs, pre):
            drow = jnp.where(lane_lo_row, decay_t2[GDN_HEADS + h0:GDN_HEADS + h0 + 1],
                             decay_t2[GDN_HEADS + h1:GDN_HEADS + h1 + 1])
            dcol = jnp.where(lane_lo, p0["dcol"], p1["dcol"])
            gamma.append(jnp.exp(jnp.where(causal, dcol - drow, -jnp.inf)))
            lhs = jnp.concatenate([jnp.concatenate([p0["kb"], p1["kb"]], axis=1),
                                   jnp.concatenate([p0["qn"], p1["qn"]], axis=1)], axis=0)
            k_bd = jnp.concatenate([jnp.concatenate([p0["kn"], zeros_hd], axis=1),
                                    jnp.concatenate([zeros_hd, p1["kn"]], axis=1)], axis=0)
            r.append(_dot_nt(lhs, k_bd))
        a_low = [jnp.where(strict, ri_[:CHUNK] * gm, 0.0) for ri_, gm in zip(r, gamma)]
        qk = [ri_[CHUNK:] * gm for ri_, gm in zip(r, gamma)]

        nmat = [-a for a in a_low]
        xpow = [_pair_dot3(a, a, lane_lo) for a in a_low]
        p2 = 2
        while 2 * p2 < CHUNK:
            prod = [_pair_dot3(jnp.concatenate([nm, xp], axis=0), xp, lane_lo)
                    for nm, xp in zip(nmat, xpow)]
            nmat = [nm + xp + pr[:CHUNK] for nm, xp, pr in zip(nmat, xpow, prod)]
            xpow = [pr[CHUNK:] for pr in prod]
            p2 *= 2
        prod = [_pair_dot3(nm, xp, lane_lo) for nm, xp in zip(nmat, xpow)]
        nmat = [nm + xp + pr for nm, xp, pr in zip(nmat, xpow, prod)]

        heads, hp, uw = [], [], []
        for (h0, h1), (p0, p1), nm in zip(pairs, pre, nmat):
            n0 = jnp.where(lane_lo, nm, 0.0)
            n1 = jnp.where(lane_lo, 0.0, nm)
            uw.append(p0["rhs"] + _dot(n0, jnp.concatenate([p0["rhs"], zeros_rhs], axis=0)))
            uw.append(p1["rhs"] + _dot(n1, jnp.concatenate([zeros_rhs, p1["rhs"]], axis=0)))
            heads += [h0, h1]
            hp += [p0, p1]
        states = [state_ref[h] for h in heads]
        ws_qs = [_dot(jnp.concatenate([uwi[:, GDN_HEAD_DIM:], p["q_dec"]], axis=0), st)
                 for uwi, p, st in zip(uw, hp, states)]
        v_new = [uwi[:, :GDN_HEAD_DIM] - wq[:CHUNK] for uwi, wq in zip(uw, ws_qs)]
        o_att = []
        for i, qki in enumerate(qk):
            v_bd = jnp.concatenate([jnp.concatenate([v_new[2 * i], zeros_hd], axis=1),
                                    jnp.concatenate([zeros_hd, v_new[2 * i + 1]], axis=1)], axis=0)
            oa = _dot(qki, v_bd)
            o_att += [oa[:, :GDN_HEAD_DIM], oa[:, GDN_HEAD_DIM:]]
        kv_upd = [_dot_tn(p["k_dec"], vn) for p, vn in zip(hp, v_new)]
        for h, p, st, wq, oa, kvu in zip(heads, hp, states, ws_qs, o_att, kv_upd):
            state_ref[h] = st * p["sdec"] + kvu
            o = wq[CHUNK:] + oa
            on = o * lax.rsqrt(jnp.mean(o * o, axis=-1, keepdims=True) + NORM_EPS) * gnorm
            zc = z_ref[0, :, p["sl"]]
            o_ref[0, :, p["sl"]] = (on * (zc * _sigmoid(zc))).astype(o_ref.dtype)

    halo_ref[...] = qkv_ref[0, CHUNK - SUBLANES:CHUNK, :]


def _gdn(proj, conv_w, a_log, dt_bias, gnorm):
    b, t, _ = proj.shape
    gp = jnp.zeros((2, LANES), F32)
    gp = gp.at[0, GDN_HEADS:2 * GDN_HEADS].set(a_log).at[1, GDN_HEADS:2 * GDN_HEADS].set(dt_bias)
    qkv_w = 3 * GDN_WIDTH
    return pl.pallas_call(
        _gdn_kernel,
        grid=(b, t // CHUNK),
        in_specs=[
            pl.BlockSpec((1, CHUNK, qkv_w), lambda i, c: (i, c, OFF_QKV // qkv_w)),
            pl.BlockSpec((1, CHUNK, GDN_WIDTH), lambda i, c: (i, c, OFF_Z // GDN_WIDTH)),
            pl.BlockSpec((1, CHUNK, LANES), lambda i, c: (i, c, OFF_BA // LANES)),
            pl.BlockSpec((CONV_K, qkv_w), lambda i, c: (0, 0)),
            pl.BlockSpec((2, LANES), lambda i, c: (0, 0)),
            pl.BlockSpec((1, GDN_HEAD_DIM), lambda i, c: (0, 0)),
        ],
        out_specs=pl.BlockSpec((1, CHUNK, GDN_WIDTH), lambda i, c: (i, c, 0)),
        out_shape=jax.ShapeDtypeStruct((b, t, GDN_WIDTH), BF16),
        scratch_shapes=[pltpu.VMEM((GDN_HEADS, GDN_HEAD_DIM, GDN_HEAD_DIM), F32),
                        pltpu.VMEM((SUBLANES, qkv_w), F32)],
        compiler_params=_params(("parallel", "arbitrary")),
        name="gdn",
    )(proj, proj, proj, conv_w, gp, gnorm.reshape(1, GDN_HEAD_DIM))


def _swa_kernel(sink_ref, q_ref, kp_ref, kc_ref, vp_ref, vc_ref, o_ref):
    nb = pl.program_id(1)
    kj = lax.broadcasted_iota(jnp.int32, (WINDOW, WINDOW), 0)
    qi = lax.broadcasted_iota(jnp.int32, (WINDOW, WINDOW), 1)
    from_cur = kj <= qi
    dist = jnp.where(from_cur, qi - kj, qi - kj + WINDOW).astype(F32)
    dist = jnp.where(jnp.logical_and(nb == 0, jnp.logical_not(from_cur)), jnp.inf, dist)
    lane_lo = qi < SWA_HEAD_DIM
    q_scale = SWA_HEAD_DIM ** -0.5
    pair_w = 2 * SWA_HEAD_DIM

    def lo_hi(tile, kv):
        swapped = pltpu.roll(tile, SWA_HEAD_DIM, axis=1)
        zero = jnp.zeros_like(tile)
        if kv % 2 == 0:
            return jnp.where(lane_lo, tile, zero), jnp.where(lane_lo, zero, swapped)
        return jnp.where(lane_lo, swapped, zero), jnp.where(lane_lo, zero, tile)

    for kv in range(SWA_KV_HEADS):
        tile_sl = slice((kv // 2) * pair_w, (kv // 2 + 1) * pair_w)
        kc_lo, kc_hi = lo_hi(kc_ref[0, :, tile_sl], kv)
        kp_lo, kp_hi = lo_hi(kp_ref[0, :, tile_sl], kv)
        kc2 = jnp.concatenate([kc_lo, kc_hi], axis=0).astype(BF16)
        kp2 = jnp.concatenate([kp_lo, kp_hi], axis=0).astype(BF16)
        v2 = jnp.concatenate([vc_ref[0, :, tile_sl], vp_ref[0, :, tile_sl]], axis=0).astype(BF16)
        row0 = (kv % 2) * SWA_HEAD_DIM
        for pp in range(SWA_GROUP // 4):
            head0 = kv * SWA_GROUP + 4 * pp
            col = head0 * SWA_HEAD_DIM
            q2 = jnp.concatenate([q_ref[0, :, col:col + pair_w],
                                  q_ref[0, :, col + pair_w:col + 2 * pair_w]], axis=0)
            q2 = (q2 * q_scale).astype(BF16)
            s_cur = lax.dot_general(kc2, q2, (((1,), (1,)), ((), ())), preferred_element_type=F32)
            s_prev = lax.dot_general(kp2, q2, (((1,), (1,)), ((), ())), preferred_element_type=F32)
            for pr in range(2):
                cs = slice(pr * WINDOW, (pr + 1) * WINDOW)
                p_cur, p_prev, inv_l = [], [], []
                for half in range(2):
                    head = head0 + 2 * pr + half
                    rs = slice(half * WINDOW, (half + 1) * WINDOW)
                    slope = 2.0 ** (-8.0 * (head + 1) / SWA_Q_HEADS)
                    s = jnp.where(from_cur, s_cur[rs, cs], s_prev[rs, cs]) - slope * dist
                    sink = sink_ref[0, head]
                    m = jnp.maximum(jnp.max(s, axis=0, keepdims=True), sink)
                    p = jnp.exp(s - m)
                    inv_l.append(1.0 / (jnp.sum(p, axis=0, keepdims=True) + jnp.exp(sink - m)))
                    pb = p.astype(BF16)
                    zero = jnp.zeros_like(pb)
                    p_cur.append(jnp.where(from_cur, pb, zero))
                    p_prev.append(jnp.where(from_cur, zero, pb))
                probs = jnp.concatenate([jnp.concatenate(p_cur, axis=1),
                                         jnp.concatenate(p_prev, axis=1)], axis=0)
                o_t = lax.dot_general(v2, probs, (((0,), (0,)), ((), ())),
                                      preferred_element_type=F32)
                o_t = o_t[row0:row0 + SWA_HEAD_DIM]
                o_pair = jnp.concatenate([o_t[:, :WINDOW] * inv_l[0],
                                          o_t[:, WINDOW:] * inv_l[1]], axis=0)
                pcol = col + pr * pair_w
                o_ref[0, :, pcol:pcol + pair_w] = jnp.transpose(o_pair).astype(o_ref.dtype)


def _swa(proj, sinks):
    b, t, _ = proj.shape
    kb = OFF_KS // SWA_KV_WIDTH
    vb = OFF_VS // SWA_KV_WIDTH
    prev = lambda i, n: jnp.maximum(n - 1, 0)
    return pl.pallas_call(
        _swa_kernel,
        grid=(b, t // WINDOW),
        in_specs=[
            pl.BlockSpec(memory_space=pltpu.SMEM),
            pl.BlockSpec((1, WINDOW, SWA_Q_WIDTH), lambda i, n: (i, n, OFF_QS // SWA_Q_WIDTH)),
            pl.BlockSpec((1, WINDOW, SWA_KV_WIDTH), lambda i, n: (i, prev(i, n), kb)),
            pl.BlockSpec((1, WINDOW, SWA_KV_WIDTH), lambda i, n: (i, n, kb)),
            pl.BlockSpec((1, WINDOW, SWA_KV_WIDTH), lambda i, n: (i, prev(i, n), vb)),
            pl.BlockSpec((1, WINDOW, SWA_KV_WIDTH), lambda i, n: (i, n, vb)),
        ],
        out_specs=pl.BlockSpec((1, WINDOW, SWA_Q_WIDTH), lambda i, n: (i, n, 0)),
        out_shape=jax.ShapeDtypeStruct((b, t, SWA_Q_WIDTH), BF16),
        compiler_params=_params(("parallel", "parallel")),
        name="swa",
    )(sinks.reshape(1, SWA_Q_HEADS), proj, proj, proj, proj, proj)


def _merge_kernel(og_ref, os_ref, wg_ref, ws_ref, gg_ref, gs_ref, o_ref):
    yg = jnp.dot(og_ref[...], wg_ref[...], preferred_element_type=F32)
    ys = jnp.dot(os_ref[...], ws_ref[...], preferred_element_type=F32)
    o_ref[...] = (_sigmoid(gg_ref[...]) * yg + _sigmoid(gs_ref[...]) * ys).astype(o_ref.dtype)


def _merge(o_gdn, o_swa, w_gdn, w_swa, proj2d, tm=1024, tn=512):
    m = o_gdn.shape[0]
    gate_blk = OFF_GATES // tn
    return pl.pallas_call(
        _merge_kernel,
        grid=(m // tm, D_MODEL // tn),
        in_specs=[
            pl.BlockSpec((tm, GDN_WIDTH), lambda i, j: (i, 0)),
            pl.BlockSpec((tm, SWA_Q_WIDTH), lambda i, j: (i, 0)),
            pl.BlockSpec((GDN_WIDTH, tn), lambda i, j: (0, j)),
            pl.BlockSpec((SWA_Q_WIDTH, tn), lambda i, j: (0, j)),
            pl.BlockSpec((tm, tn), lambda i, j: (i, gate_blk + j)),
            pl.BlockSpec((tm, tn), lambda i, j: (i, gate_blk + D_MODEL // tn + j)),
        ],
        out_specs=pl.BlockSpec((tm, tn), lambda i, j: (i, j)),
        out_shape=jax.ShapeDtypeStruct((m, D_MODEL), BF16),
        compiler_params=_params(("parallel", "parallel")),
        name="merge",
    )(o_gdn, o_swa, w_gdn, w_swa, proj2d, proj2d)


def _pack_w_in(w_in):
    seg = lambda r: w_in[:, :, r[0]:r[1]]
    pad = jnp.zeros(w_in.shape[:2] + (LANES - 2 * GDN_HEADS,), w_in.dtype)
    parts = [seg(SRC_QKVZ), seg(SRC_QS), seg(SRC_GATES), seg(SRC_KS), seg(SRC_VS), seg(SRC_BA), pad]
    return jnp.concatenate(parts, axis=-1).astype(BF16)


def kernel(x, norm1_g, w_in, conv_w, a_log, dt_bias, gdn_norm_g, attn_sinks, w_branch_gdn,
           w_branch_swa, w_out, norm2_g, w_ff_up, w_ff_down, final_norm_g):
    b, t, d = x.shape
    depth = w_in.shape[0]
    n = b * t
    w_in_p = _pack_w_in(w_in)
    w_gdn = w_branch_gdn.astype(BF16)
    w_swa = w_branch_swa.astype(BF16)
    w_o = w_out.astype(BF16)
    w_up = w_ff_up.astype(BF16)
    w_down = w_ff_down.astype(BF16)

    xf = x.reshape(n, d)
    for l in range(depth):
        h = _rmsnorm(xf, norm1_g[l], BF16)
        proj = _matmul(h, w_in_p[l], out_dtype=F32, tm=1024, tn=1152, name="in_proj")
        proj3 = proj.reshape(b, t, PROJ_WIDTH)
        o_gdn = _gdn(proj3, conv_w[l], a_log[l], dt_bias[l], gdn_norm_g[l]).reshape(n, GDN_WIDTH)
        o_swa = _swa(proj3, attn_sinks[l]).reshape(n, SWA_Q_WIDTH)
        merged = _merge(o_gdn, o_swa, w_gdn[l], w_swa[l], proj)
        xf = _matmul(merged, w_o[l], out_dtype=F32, tm=1024, tn=1024, epilogue="residual", res=xf,
                     name="out_proj")
        h2 = _rmsnorm(xf, norm2_g[l], BF16)
        act = _matmul(h2, w_up[l], out_dtype=BF16, tm=1024, tn=1024, epilogue="relu2", name="ff_up")
        xf = _matmul(act, w_down[l], out_dtype=F32, tm=1024, tn=1024, tk=2048, epilogue="residual",
                     res=xf, name="ff_down")
    return _rmsnorm(xf, final_norm_g, F32).reshape(b, t, d)
```

```python
import functools

import jax
import jax.numpy as jnp
from jax import lax
from jax.experimental import pallas as pl
from jax.experimental.pallas import tpu as pltpu

F32 = jnp.float32
BF16 = jnp.bfloat16

D_MODEL = 2048
GDN_HEAD_DIM = 128
GDN_HEADS = 16
GDN_WIDTH = GDN_HEADS * GDN_HEAD_DIM
CONV_K = 4
CHUNK = 64
GDN_HEAD_GROUP = 16
SWA_HEAD_DIM = 64
SWA_Q_HEADS = 32
SWA_KV_HEADS = 4
SWA_GROUP = SWA_Q_HEADS // SWA_KV_HEADS
SWA_Q_WIDTH = SWA_Q_HEADS * SWA_HEAD_DIM
SWA_KV_WIDTH = SWA_KV_HEADS * SWA_HEAD_DIM
WINDOW = 128
D_FF = 4 * D_MODEL
NORM_EPS = 1e-6

LANES = 128
SUBLANES = 8
VMEM_LIMIT_BYTES = 56 * 1024 * 1024

W_CAST_ROWS = 256
ROW_TILE = 1024
COL_TILE = 1024

QKVZ_WIDTH = 4 * GDN_WIDTH
SRC_BA = QKVZ_WIDTH
SRC_REST = SRC_BA + 2 * GDN_HEADS
REST_SHIFT = SRC_REST - SRC_BA
REST_WIDTH = SWA_Q_WIDTH + 2 * SWA_KV_WIDTH + 2 * D_MODEL
D_IN = SRC_REST + REST_WIDTH
ROFF_QS = 0
ROFF_KS = ROFF_QS + SWA_Q_WIDTH
ROFF_VS = ROFF_KS + SWA_KV_WIDTH
ROFF_GATES = ROFF_VS + SWA_KV_WIDTH


def _params(semantics):
    return pltpu.CompilerParams(dimension_semantics=semantics, vmem_limit_bytes=VMEM_LIMIT_BYTES)


def _rmsnorm_kernel(x_ref, g_ref, o_ref):
    x = x_ref[...]
    ms = jnp.mean(x * x, axis=-1, keepdims=True)
    o_ref[...] = (x * lax.rsqrt(ms + NORM_EPS) * g_ref[...]).astype(o_ref.dtype)


def _rmsnorm(x, gain, out_dtype, rows=512):
    n, d = x.shape
    return pl.pallas_call(
        _rmsnorm_kernel,
        grid=(n // rows,),
        in_specs=[pl.BlockSpec((rows, d), lambda i: (i, 0)),
                  pl.BlockSpec((1, d), lambda i: (0, 0))],
        out_specs=pl.BlockSpec((rows, d), lambda i: (i, 0)),
        out_shape=jax.ShapeDtypeStruct((n, d), out_dtype),
        compiler_params=_params(("parallel",)),
        name="rmsnorm",
    )(x, gain.reshape(1, d))


def _mm_kernel(*refs, nk, epilogue):
    if epilogue == "residual":
        a_ref, w_ref, r_ref, o_ref = refs[:4]
        rest = refs[4:]
    else:
        a_ref, w_ref, o_ref = refs[:3]
        r_ref = None
        rest = refs[3:]

    def finish(acc):
        if epilogue == "relu2":
            acc = jnp.square(jnp.maximum(acc, 0.0))
        elif epilogue == "residual":
            acc = r_ref[...] + acc
        o_ref[...] = acc.astype(o_ref.dtype)

    part = jnp.dot(a_ref[...], w_ref[...], preferred_element_type=F32)
    if nk == 1:
        finish(part)
        return
    acc_ref = rest[0]
    k = pl.program_id(2)

    @pl.when(k == 0)
    def _():
        acc_ref[...] = part

    @pl.when(jnp.logical_and(k > 0, k < nk - 1))
    def _():
        acc_ref[...] += part

    @pl.when(k == nk - 1)
    def _():
        finish(acc_ref[...] + part)


def _matmul(a, w, *, out_dtype, tm, tn, tk=None, epilogue="store", res=None, name="matmul"):
    m, kdim = a.shape
    n = w.shape[1]
    tk = kdim if tk is None else tk
    nk = kdim // tk
    grid = (m // tm, pl.cdiv(n, tn), nk)
    in_specs = [pl.BlockSpec((tm, tk), lambda i, j, k: (i, k)),
                pl.BlockSpec((tk, tn), lambda i, j, k: (k, j))]
    args = [a, w]
    if epilogue == "residual":
        in_specs.append(pl.BlockSpec((tm, tn), lambda i, j, k: (i, j)))
        args.append(res)
    scratch = [pltpu.VMEM((tm, tn), F32)] if nk > 1 else []
    return pl.pallas_call(
        functools.partial(_mm_kernel, nk=nk, epilogue=epilogue),
        grid=grid,
        in_specs=in_specs,
        out_specs=pl.BlockSpec((tm, tn), lambda i, j, k: (i, j)),
        out_shape=jax.ShapeDtypeStruct((m, n), out_dtype),
        scratch_shapes=scratch,
        compiler_params=_params(("parallel", "parallel", "arbitrary")),
        name=name,
    )(*args)


def _mm_ws_kernel(*refs, epilogue, shift):
    if shift:
        a_ref, w_ref, trail_ref, o_ref, wbf_ref = refs
    else:
        a_ref, w_ref, o_ref, wbf_ref = refs
        trail_ref = None
    i = pl.program_id(1)
    kdim, tn = wbf_ref.shape

    @pl.when(i == 0)
    def _():
        for r in range(0, kdim, W_CAST_ROWS):
            rows = slice(r, r + W_CAST_ROWS)
            w = w_ref[rows, :]
            if shift:
                w = jnp.concatenate([w, trail_ref[rows, :]], axis=1)[:, shift:shift + tn]
            wbf_ref[rows, :] = w.astype(BF16)

    acc = jnp.dot(a_ref[...], wbf_ref[...], preferred_element_type=F32)
    if epilogue == "relu2":
        o_ref[...] = jnp.square(jnp.maximum(acc, 0.0)).astype(o_ref.dtype)
    else:
        o_ref[...] = acc.astype(o_ref.dtype)


def _matmul_ws(a, w, layer, *, col0, n_out, out_dtype, tm, tn, epilogue="store", shift=0,
               name="matmul_ws"):
    m, kdim = a.shape
    nw = w.shape[2]
    blk0 = col0 // tn
    in_specs = [pl.BlockSpec((tm, kdim), lambda j, i: (i, 0)),
                pl.BlockSpec((None, kdim, tn), lambda j, i: (layer, 0, blk0 + j))]
    args = [a, w]
    if shift:
        last_tile = (nw - 1) // LANES
        in_specs.append(pl.BlockSpec(
            (None, kdim, LANES),
            lambda j, i: (layer, 0, jnp.minimum((col0 + (j + 1) * tn) // LANES, last_tile))))
        args.append(w)
    return pl.pallas_call(
        functools.partial(_mm_ws_kernel, epilogue=epilogue, shift=shift),
        grid=(pl.cdiv(n_out, tn), m // tm),
        in_specs=in_specs,
        out_specs=pl.BlockSpec((tm, tn), lambda j, i: (i, j)),
        out_shape=jax.ShapeDtypeStruct((m, n_out), out_dtype),
        scratch_shapes=[pltpu.VMEM((kdim, tn), BF16)],
        compiler_params=_params(("parallel", "arbitrary")),
        name=name,
    )(*args)


def _sigmoid(x):
    return 1.0 / (1.0 + jnp.exp(-x))


def _softplus(x):
    return jnp.maximum(x, 0.0) + jnp.log(1.0 + jnp.exp(-jnp.abs(x)))


def _cumsum_rows(x):
    rows = x.shape[0]
    row = lax.broadcasted_iota(jnp.int32, x.shape, 0)
    s = 1
    while s < rows:
        x = x + jnp.where(row >= s, pltpu.roll(x, s, axis=0), 0.0)
        s *= 2
    return x


def _shift_rows(x, halo, s):
    xr = pltpu.roll(x, s, axis=0)
    hr = pltpu.roll(halo, s, axis=0)
    row = lax.broadcasted_iota(jnp.int32, halo.shape, 0)
    first = jnp.where(row < s, hr, xr[:SUBLANES])
    return jnp.concatenate([first, xr[SUBLANES:]], axis=0)


def _conv_silu(x, halo, w):
    acc = x * w[CONV_K - 1:CONV_K]
    for s in range(1, CONV_K):
        acc = acc + _shift_rows(x, halo, s) * w[CONV_K - 1 - s:CONV_K - s]
    return acc * _sigmoid(acc)


def _dot(a, b):
    return jnp.dot(a.astype(BF16), b.astype(BF16), preferred_element_type=F32)


def _dot_nt(a, b):
    return lax.dot_general(a.astype(BF16), b.astype(BF16), (((1,), (1,)), ((), ())),
                           preferred_element_type=F32)


def _dot_tn(a, b):
    return lax.dot_general(a.astype(BF16), b.astype(BF16), (((0,), (0,)), ((), ())),
                           preferred_element_type=F32)


def _split_bf16(x):
    hi = x.astype(BF16)
    return hi, (x - hi.astype(F32)).astype(BF16)


def _block_diag_pair(xp, lane_lo):
    zero = jnp.zeros_like(xp)
    return jnp.concatenate([jnp.where(lane_lo, xp, zero), jnp.where(lane_lo, zero, xp)], axis=0)


def _pair_dot3(l, xp, lane_lo):
    l_hi, l_lo = _split_bf16(l)
    x_hi, x_lo = _split_bf16(xp)
    bd_hi = _block_diag_pair(x_hi, lane_lo)
    bd_lo = _block_diag_pair(x_lo, lane_lo)
    lhs = jnp.concatenate([l_hi, l_lo, l_hi], axis=1)
    rhs = jnp.concatenate([bd_hi, bd_hi, bd_lo], axis=0)
    return jnp.dot(lhs, rhs, preferred_element_type=F32)


def _gdn_kernel(qkv_ref, z_ref, ba_ref, convw_ref, gp_ref, gnorm_ref, o_ref, state_ref, halo_ref):
    c = pl.program_id(1)

    @pl.when(c == 0)
    def _():
        state_ref[...] = jnp.zeros_like(state_ref)
        halo_ref[...] = jnp.zeros_like(halo_ref)

    ba = ba_ref[0]
    beta_all = _sigmoid(ba)
    g_all = -jnp.exp(gp_ref[0:1]) * _softplus(ba + gp_ref[1:2])
    decay_all = _cumsum_rows(g_all)
    decay_t2 = jnp.transpose(jnp.concatenate([decay_all, decay_all], axis=0))

    ri = lax.broadcasted_iota(jnp.int32, (CHUNK, 2 * CHUNK), 0)
    ci = lax.broadcasted_iota(jnp.int32, (CHUNK, 2 * CHUNK), 1)
    lane_lo = ci < CHUNK
    cj = jnp.where(lane_lo, ci, ci - CHUNK)
    causal = ri >= cj
    strict = ri > cj
    lane_lo_row = lane_lo[0:1]
    gnorm = gnorm_ref[...]
    q_scale = GDN_HEAD_DIM ** -0.5
    zeros_hd = jnp.zeros((CHUNK, GDN_HEAD_DIM), F32)
    zeros_rhs = jnp.zeros((CHUNK, 2 * GDN_HEAD_DIM), F32)

    def head_inputs(h):
        lo = h * GDN_HEAD_DIM
        sl_q = slice(lo, lo + GDN_HEAD_DIM)
        sl_k = slice(GDN_WIDTH + lo, GDN_WIDTH + lo + GDN_HEAD_DIM)
        sl_v = slice(2 * GDN_WIDTH + lo, 2 * GDN_WIDTH + lo + GDN_HEAD_DIM)
        qc = _conv_silu(qkv_ref[0, :, sl_q], halo_ref[:, sl_q], convw_ref[:, sl_q])
        kc = _conv_silu(qkv_ref[0, :, sl_k], halo_ref[:, sl_k], convw_ref[:, sl_k])
        vc = _conv_silu(qkv_ref[0, :, sl_v], halo_ref[:, sl_v], convw_ref[:, sl_v])
        qn = qc * (lax.rsqrt(jnp.sum(qc * qc, axis=-1, keepdims=True) + NORM_EPS) * q_scale)
        kn = kc * lax.rsqrt(jnp.sum(kc * kc, axis=-1, keepdims=True) + NORM_EPS)
        beta = beta_all[:, h:h + 1]
        dcol = decay_all[:, GDN_HEADS + h:GDN_HEADS + h + 1]
        dlast = dcol[CHUNK - 1:CHUNK]
        edec = jnp.exp(dcol)
        kb = kn * beta
        return dict(sl=sl_q, qn=qn, kn=kn, kb=kb, dcol=dcol, q_dec=qn * edec,
                    k_dec=kn * jnp.exp(dlast - dcol), sdec=jnp.exp(dlast),
                    rhs=jnp.concatenate([vc * beta, kb * edec], axis=1))

    for g0 in range(0, GDN_HEADS, GDN_HEAD_GROUP):
        pairs = [(h, h + 1) for h in range(g0, g0 + GDN_HEAD_GROUP, 2)]
        pre = [(head_inputs(h0), head_inputs(h1)) for h0, h1 in pairs]

        gamma, r = [], []
        for (h0, h1), (p0, p1) in zip(pairs, pre):
            drow = jnp.where(lane_lo_row, decay_t2[GDN_HEADS + h0:GDN_HEADS + h0 + 1],
                             decay_t2[GDN_HEADS + h1:GDN_HEADS + h1 + 1])
            dcol = jnp.where(lane_lo, p0["dcol"], p1["dcol"])
            gamma.append(jnp.exp(jnp.where(causal, dcol - drow, -jnp.inf)))
            lhs = jnp.concatenate([jnp.concatenate([p0["kb"], p1["kb"]], axis=1),
                                   jnp.concatenate([p0["qn"], p1["qn"]], axis=1)], axis=0)
            k_bd = jnp.concatenate([jnp.concatenate([p0["kn"], zeros_hd], axis=1),
                                    jnp.concatenate([zeros_hd, p1["kn"]], axis=1)], axis=0)
            r.append(_dot_nt(lhs, k_bd))
        a_low = [jnp.where(strict, ri_[:CHUNK] * gm, 0.0) for ri_, gm in zip(r, gamma)]
        qk = [ri_[CHUNK:] * gm for ri_, gm in zip(r, gamma)]

        nmat = [-a for a in a_low]
        xpow = [_pair_dot3(a, a, lane_lo) for a in a_low]
        p2 = 2
        while 2 * p2 < CHUNK:
            prod = [_pair_dot3(jnp.concatenate([nm, xp], axis=0), xp, lane_lo)
                    for nm, xp in zip(nmat, xpow)]
            nmat = [nm + xp + pr[:CHUNK] for nm, xp, pr in zip(nmat, xpow, prod)]
            xpow = [pr[CHUNK:] for pr in prod]
            p2 *= 2
        prod = [_pair_dot3(nm, xp, lane_lo) for nm, xp in zip(nmat, xpow)]
        nmat = [nm + xp + pr for nm, xp, pr in zip(nmat, xpow, prod)]

        heads, hp, uw = [], [], []
        for (h0, h1), (p0, p1), nm in zip(pairs, pre, nmat):
            n0 = jnp.where(lane_lo, nm, 0.0)
            n1 = jnp.where(lane_lo, 0.0, nm)
            uw.append(p0["rhs"] + _dot(n0, jnp.concatenate([p0["rhs"], zeros_rhs], axis=0)))
            uw.append(p1["rhs"] + _dot(n1, jnp.concatenate([zeros_rhs, p1["rhs"]], axis=0)))
            heads += [h0, h1]
            hp += [p0, p1]
        states = [state_ref[h] for h in heads]
        ws_qs = [_dot(jnp.concatenate([uwi[:, GDN_HEAD_DIM:], p["q_dec"]], axis=0), st)
                 for uwi, p, st in zip(uw, hp, states)]
        v_new = [uwi[:, :GDN_HEAD_DIM] - wq[:CHUNK] for uwi, wq in zip(uw, ws_qs)]
        o_att = []
        for i, qki in enumerate(qk):
            v_bd = jnp.concatenate([jnp.concatenate([v_new[2 * i], zeros_hd], axis=1),
                                    jnp.concatenate([zeros_hd, v_new[2 * i + 1]], axis=1)], axis=0)
            oa = _dot(qki, v_bd)
            o_att += [oa[:, :GDN_HEAD_DIM], oa[:, GDN_HEAD_DIM:]]
        kv_upd = [_dot_tn(p["k_dec"], vn) for p, vn in zip(hp, v_new)]
        for h, p, st, wq, oa, kvu in zip(heads, hp, states, ws_qs, o_att, kv_upd):
            state_ref[h] = st * p["sdec"] + kvu
            o = wq[CHUNK:] + oa
            on = o * lax.rsqrt(jnp.mean(o * o, axis=-1, keepdims=True) + NORM_EPS) * gnorm
            zc = z_ref[0, :, p["sl"]]
            o_ref[0, :, p["sl"]] = (on * (zc * _sigmoid(zc))).astype(o_ref.dtype)

    halo_ref[...] = qkv_ref[0, CHUNK - SUBLANES:CHUNK, :]


def _gdn(qkvz, ba, conv_w, a_log, dt_bias, gnorm):
    b, t, _ = qkvz.shape
    gp = jnp.zeros((2, LANES), F32)
    gp = gp.at[0, GDN_HEADS:2 * GDN_HEADS].set(a_log).at[1, GDN_HEADS:2 * GDN_HEADS].set(dt_bias)
    qkv_w = 3 * GDN_WIDTH
    return pl.pallas_call(
        _gdn_kernel,
        grid=(b, t // CHUNK),
        in_specs=[
            pl.BlockSpec((1, CHUNK, qkv_w), lambda i, c: (i, c, 0)),
            pl.BlockSpec((1, CHUNK, GDN_WIDTH), lambda i, c: (i, c, qkv_w // GDN_WIDTH)),
            pl.BlockSpec((1, CHUNK, LANES), lambda i, c: (i, c, 0)),
            pl.BlockSpec((CONV_K, qkv_w), lambda i, c: (0, 0)),
            pl.BlockSpec((2, LANES), lambda i, c: (0, 0)),
            pl.BlockSpec((1, GDN_HEAD_DIM), lambda i, c: (0, 0)),
        ],
        out_specs=pl.BlockSpec((1, CHUNK, GDN_WIDTH), lambda i, c: (i, c, 0)),
        out_shape=jax.ShapeDtypeStruct((b, t, GDN_WIDTH), BF16),
        scratch_shapes=[pltpu.VMEM((GDN_HEADS, GDN_HEAD_DIM, GDN_HEAD_DIM), F32),
                        pltpu.VMEM((SUBLANES, qkv_w), F32)],
        compiler_params=_params(("parallel", "arbitrary")),
        name="gdn",
    )(qkvz, qkvz, ba, conv_w, gp, gnorm.reshape(1, GDN_HEAD_DIM))


def _swa_kernel(sink_ref, q_ref, kp_ref, kc_ref, vp_ref, vc_ref, o_ref):
    nb = pl.program_id(1)
    kj = lax.broadcasted_iota(jnp.int32, (WINDOW, WINDOW), 0)
    qi = lax.broadcasted_iota(jnp.int32, (WINDOW, WINDOW), 1)
    from_cur = kj <= qi
    dist = jnp.where(from_cur, qi - kj, qi - kj + WINDOW).astype(F32)
    dist = jnp.where(jnp.logical_and(nb == 0, jnp.logical_not(from_cur)), jnp.inf, dist)
    lane_lo = qi < SWA_HEAD_DIM
    q_scale = SWA_HEAD_DIM ** -0.5
    pair_w = 2 * SWA_HEAD_DIM

    def lo_hi(tile, kv):
        swapped = pltpu.roll(tile, SWA_HEAD_DIM, axis=1)
        zero = jnp.zeros_like(tile)
        if kv % 2 == 0:
            return jnp.where(lane_lo, tile, zero), jnp.where(lane_lo, zero, swapped)
        return jnp.where(lane_lo, swapped, zero), jnp.where(lane_lo, zero, tile)

    batches = []
    for kv in range(SWA_KV_HEADS):
        tile_sl = slice((kv // 2) * pair_w, (kv // 2 + 1) * pair_w)
        kc_lo, kc_hi = lo_hi(kc_ref[0, :, tile_sl], kv)
        kp_lo, kp_hi = lo_hi(kp_ref[0, :, tile_sl], kv)
        kc2 = jnp.concatenate([kc_lo, kc_hi], axis=0).astype(BF16)
        kp2 = jnp.concatenate([kp_lo, kp_hi], axis=0).astype(BF16)
        v2 = jnp.concatenate([vc_ref[0, :, tile_sl], vp_ref[0, :, tile_sl]], axis=0).astype(BF16)
        row0 = (kv % 2) * SWA_HEAD_DIM
        for pp in range(SWA_GROUP // 4):
            head0 = kv * SWA_GROUP + 4 * pp
            col = head0 * SWA_HEAD_DIM
            q2 = jnp.concatenate([q_ref[0, :, col:col + pair_w],
                                  q_ref[0, :, col + pair_w:col + 2 * pair_w]], axis=0)
            q2 = (q2 * q_scale).astype(BF16)
            s_cur = lax.dot_general(kc2, q2, (((1,), (1,)), ((), ())), preferred_element_type=F32)
            s_prev = lax.dot_general(kp2, q2, (((1,), (1,)), ((), ())), preferred_element_type=F32)
            batches.append((head0, col, v2, row0, s_cur, s_prev))

    pairs = []
    for head0, col, v2, row0, s_cur, s_prev in batches:
        for pr in range(2):
            cs = slice(pr * WINDOW, (pr + 1) * WINDOW)
            p_cur, p_prev, inv_l = [], [], []
            for half in range(2):
                head = head0 + 2 * pr + half
                rs = slice(half * WINDOW, (half + 1) * WINDOW)
                slope = 2.0 ** (-8.0 * (head + 1) / SWA_Q_HEADS)
                s = jnp.where(from_cur, s_cur[rs, cs], s_prev[rs, cs]) - slope * dist
                sink = sink_ref[0, head]
                m = jnp.maximum(jnp.max(s, axis=0, keepdims=True), sink)
                p = jnp.exp(s - m)
                inv_l.append(1.0 / (jnp.sum(p, axis=0, keepdims=True) + jnp.exp(sink - m)))
                pb = p.astype(BF16)
                zero = jnp.zeros_like(pb)
                p_cur.append(jnp.where(from_cur, pb, zero))
                p_prev.append(jnp.where(from_cur, zero, pb))
            probs = jnp.concatenate([jnp.concatenate(p_cur, axis=1),
                                     jnp.concatenate(p_prev, axis=1)], axis=0)
            pairs.append((col + pr * pair_w, v2, row0, probs, inv_l))

    outs = [lax.dot_general(v2, probs, (((0,), (0,)), ((), ())), preferred_element_type=F32)
            for _, v2, _, probs, _ in pairs]
    for (pcol, _, row0, _, inv_l), o_t in zip(pairs, outs):
        o_t = o_t[row0:row0 + SWA_HEAD_DIM]
        o_pair = jnp.concatenate([o_t[:, :WINDOW] * inv_l[0],
                                  o_t[:, WINDOW:] * inv_l[1]], axis=0)
        o_ref[0, :, pcol:pcol + pair_w] = jnp.transpose(o_pair).astype(o_ref.dtype)


def _swa(rest, sinks):
    b, t, _ = rest.shape
    kb = ROFF_KS // SWA_KV_WIDTH
    vb = ROFF_VS // SWA_KV_WIDTH
    prev = lambda n: jnp.maximum(n - 1, 0)
    return pl.pallas_call(
        _swa_kernel,
        grid=(b, t // WINDOW),
        in_specs=[
            pl.BlockSpec(memory_space=pltpu.SMEM),
            pl.BlockSpec((1, WINDOW, SWA_Q_WIDTH), lambda i, n: (i, n, ROFF_QS // SWA_Q_WIDTH)),
            pl.BlockSpec((1, WINDOW, SWA_KV_WIDTH), lambda i, n: (i, prev(n), kb)),
            pl.BlockSpec((1, WINDOW, SWA_KV_WIDTH), lambda i, n: (i, n, kb)),
            pl.BlockSpec((1, WINDOW, SWA_KV_WIDTH), lambda i, n: (i, prev(n), vb)),
            pl.BlockSpec((1, WINDOW, SWA_KV_WIDTH), lambda i, n: (i, n, vb)),
        ],
        out_specs=pl.BlockSpec((1, WINDOW, SWA_Q_WIDTH), lambda i, n: (i, n, 0)),
        out_shape=jax.ShapeDtypeStruct((b, t, SWA_Q_WIDTH), BF16),
        compiler_params=_params(("parallel", "parallel")),
        name="swa",
    )(sinks.reshape(1, SWA_Q_HEADS), rest, rest, rest, rest, rest)


def _merge_kernel(og_ref, os_ref, wg_ref, ws_ref, gg_ref, gs_ref, o_ref):
    yg = jnp.dot(og_ref[...], wg_ref[...], preferred_element_type=F32)
    ys = jnp.dot(os_ref[...], ws_ref[...], preferred_element_type=F32)
    o_ref[...] = (_sigmoid(gg_ref[...]) * yg + _sigmoid(gs_ref[...]) * ys).astype(o_ref.dtype)


def _merge(o_gdn, o_swa, w_gdn, w_swa, rest2d, tm, tn=512):
    m = o_gdn.shape[0]
    gate_blk = ROFF_GATES // tn
    return pl.pallas_call(
        _merge_kernel,
        grid=(m // tm, D_MODEL // tn),
        in_specs=[
            pl.BlockSpec((tm, GDN_WIDTH), lambda i, j: (i, 0)),
            pl.BlockSpec((tm, SWA_Q_WIDTH), lambda i, j: (i, 0)),
            pl.BlockSpec((GDN_WIDTH, tn), lambda i, j: (0, j)),
            pl.BlockSpec((SWA_Q_WIDTH, tn), lambda i, j: (0, j)),
            pl.BlockSpec((tm, tn), lambda i, j: (i, gate_blk + j)),
            pl.BlockSpec((tm, tn), lambda i, j: (i, gate_blk + D_MODEL // tn + j)),
        ],
        out_specs=pl.BlockSpec((tm, tn), lambda i, j: (i, j)),
        out_shape=jax.ShapeDtypeStruct((m, D_MODEL), BF16),
        compiler_params=_params(("parallel", "parallel")),
        name="merge",
    )(o_gdn, o_swa, w_gdn, w_swa, rest2d, rest2d)


def kernel(x, norm1_g, w_in, conv_w, a_log, dt_bias, gdn_norm_g, attn_sinks, w_branch_gdn,
           w_branch_swa, w_out, norm2_g, w_ff_up, w_ff_down, final_norm_g):
    b, t, d = x.shape
    depth = w_in.shape[0]
    n = b * t
    tm, tn = ROW_TILE, COL_TILE
    w_gdn = w_branch_gdn.astype(BF16)
    w_swa = w_branch_swa.astype(BF16)
    w_o = w_out.astype(BF16)
    w_down = w_ff_down.astype(BF16)

    xf = x.reshape(n, d)
    for l in range(depth):
        h = _rmsnorm(xf, norm1_g[l], BF16)
        qkvz = _matmul_ws(h, w_in, l, col0=0, n_out=QKVZ_WIDTH, out_dtype=F32, tm=tm, tn=tn,
                          name="in_proj_qkvz")
        ba = _matmul_ws(h, w_in, l, col0=SRC_BA, n_out=LANES, out_dtype=F32, tm=tm, tn=LANES,
                        name="in_proj_ba")
        rest = _matmul_ws(h, w_in, l, col0=SRC_BA, n_out=REST_WIDTH, out_dtype=F32, tm=tm, tn=tn,
                          shift=REST_SHIFT, name="in_proj_rest")
        o_gdn = _gdn(qkvz.reshape(b, t, QKVZ_WIDTH), ba.reshape(b, t, LANES), conv_w[l], a_log[l],
                     dt_bias[l], gdn_norm_g[l]).reshape(n, GDN_WIDTH)
        o_swa = _swa(rest.reshape(b, t, REST_WIDTH), attn_sinks[l]).reshape(n, SWA_Q_WIDTH)
        merged = _merge(o_gdn, o_swa, w_gdn[l], w_swa[l], rest, tm=tm)
        xf = _matmul(merged, w_o[l], out_dtype=F32, tm=tm, tn=tn, epilogue="residual", res=xf,
                     name="out_proj")
        h2 = _rmsnorm(xf, norm2_g[l], BF16)
        act = _matmul_ws(h2, w_ff_up, l, col0=0, n_out=D_FF, out_dtype=BF16, tm=tm, tn=tn,
                         epilogue="relu2", name="ff_up")
        xf = _matmul(act, w_down[l], out_dtype=F32, tm=tm, tn=tn, tk=2048, epilogue="residual",
                     res=xf, name="ff_down")
    return _rmsnorm(xf, final_norm_g, F32).reshape(b, t, d)
```

```python
import functools

import jax
import jax.numpy as jnp
from jax import lax
from jax.experimental import pallas as pl
from jax.experimental.pallas import tpu as pltpu

F32 = jnp.float32
BF16 = jnp.bfloat16

D_MODEL = 2048
GDN_HEAD_DIM = 128
GDN_HEADS = 16
GDN_WIDTH = GDN_HEADS * GDN_HEAD_DIM
CONV_K = 4
CHUNK = 64
GDN_HEAD_GROUP = 16
SWA_HEAD_DIM = 64
SWA_Q_HEADS = 32
SWA_KV_HEADS = 4
SWA_GROUP = SWA_Q_HEADS // SWA_KV_HEADS
SWA_Q_WIDTH = SWA_Q_HEADS * SWA_HEAD_DIM
SWA_KV_WIDTH = SWA_KV_HEADS * SWA_HEAD_DIM
WINDOW = 128
D_FF = 4 * D_MODEL
NORM_EPS = 1e-6

LANES = 128
SUBLANES = 8
VMEM_LIMIT_BYTES = 56 * 1024 * 1024

W_CAST_ROWS = 256
ROW_TILE = 1024
COL_TILE = 1024

QKVZ_WIDTH = 4 * GDN_WIDTH
SRC_BA = QKVZ_WIDTH
SRC_REST = SRC_BA + 2 * GDN_HEADS
REST_SHIFT = SRC_REST - SRC_BA
REST_WIDTH = SWA_Q_WIDTH + 2 * SWA_KV_WIDTH + 2 * D_MODEL
D_IN = SRC_REST + REST_WIDTH
ROFF_QS = 0
ROFF_KS = ROFF_QS + SWA_Q_WIDTH
ROFF_VS = ROFF_KS + SWA_KV_WIDTH
ROFF_GATES = ROFF_VS + SWA_KV_WIDTH


def _params(semantics):
    return pltpu.CompilerParams(dimension_semantics=semantics, vmem_limit_bytes=VMEM_LIMIT_BYTES)


def _rmsnorm_kernel(x_ref, g_ref, o_ref):
    x = x_ref[...]
    ms = jnp.mean(x * x, axis=-1, keepdims=True)
    o_ref[...] = (x * lax.rsqrt(ms + NORM_EPS) * g_ref[...]).astype(o_ref.dtype)


def _rmsnorm(x, gain, out_dtype, rows=512):
    n, d = x.shape
    return pl.pallas_call(
        _rmsnorm_kernel,
        grid=(n // rows,),
        in_specs=[pl.BlockSpec((rows, d), lambda i: (i, 0)),
                  pl.BlockSpec((1, d), lambda i: (0, 0))],
        out_specs=pl.BlockSpec((rows, d), lambda i: (i, 0)),
        out_shape=jax.ShapeDtypeStruct((n, d), out_dtype),
        compiler_params=_params(("parallel",)),
        name="rmsnorm",
    )(x, gain.reshape(1, d))


def _mm_kernel(*refs, nk, epilogue):
    if epilogue == "residual":
        a_ref, w_ref, r_ref, o_ref = refs[:4]
        rest = refs[4:]
    else:
        a_ref, w_ref, o_ref = refs[:3]
        r_ref = None
        rest = refs[3:]

    def finish(acc):
        if epilogue == "relu2":
            acc = jnp.square(jnp.maximum(acc, 0.0))
        elif epilogue == "residual":
            acc = r_ref[...] + acc
        o_ref[...] = acc.astype(o_ref.dtype)

    part = jnp.dot(a_ref[...], w_ref[...], preferred_element_type=F32)
    if nk == 1:
        finish(part)
        return
    acc_ref = rest[0]
    k = pl.program_id(2)

    @pl.when(k == 0)
    def _():
        acc_ref[...] = part

    @pl.when(jnp.logical_and(k > 0, k < nk - 1))
    def _():
        acc_ref[...] += part

    @pl.when(k == nk - 1)
    def _():
        finish(acc_ref[...] + part)


def _matmul(a, w, layer, *, out_dtype, tm, tn, tk=None, epilogue="store", res=None, name="matmul"):
    m, kdim = a.shape
    n = w.shape[2]
    tk = kdim if tk is None else tk
    nk = kdim // tk
    grid = (m // tm, pl.cdiv(n, tn), nk)
    in_specs = [pl.BlockSpec((tm, tk), lambda i, j, k: (i, k)),
                pl.BlockSpec((None, tk, tn), lambda i, j, k: (layer, k, j))]
    args = [a, w]
    if epilogue == "residual":
        in_specs.append(pl.BlockSpec((tm, tn), lambda i, j, k: (i, j)))
        args.append(res)
    scratch = [pltpu.VMEM((tm, tn), F32)] if nk > 1 else []
    return pl.pallas_call(
        functools.partial(_mm_kernel, nk=nk, epilogue=epilogue),
        grid=grid,
        in_specs=in_specs,
        out_specs=pl.BlockSpec((tm, tn), lambda i, j, k: (i, j)),
        out_shape=jax.ShapeDtypeStruct((m, n), out_dtype),
        scratch_shapes=scratch,
        compiler_params=_params(("parallel", "parallel", "arbitrary")),
        name=name,
    )(*args)


def _mm_ws_kernel(*refs, epilogue, shift, transposed):
    refs = list(refs)
    a_ref, w_ref = refs[:2]
    pos = 2
    trail_ref = res_ref = None
    if shift:
        trail_ref = refs[pos]
        pos += 1
    if epilogue == "residual":
        res_ref = refs[pos]
        pos += 1
    o_ref, wbf_ref = refs[pos], refs[pos + 1]
    i = pl.program_id(1)
    n_rows = wbf_ref.shape[0]

    @pl.when(i == 0)
    def _():
        body = n_rows - shift
        for r in range(0, body, W_CAST_ROWS):
            rows = min(W_CAST_ROWS, body - r)
            wbf_ref[r:r + rows, :] = w_ref[r + shift:r + shift + rows, :].astype(BF16)
        if shift:
            wbf_ref[body:, :] = trail_ref[...].astype(BF16)

    if transposed:
        acc = lax.dot_general(a_ref[...], wbf_ref[...], (((1,), (1,)), ((), ())),
                              preferred_element_type=F32)
    else:
        acc = jnp.dot(a_ref[...], wbf_ref[...], preferred_element_type=F32)
    if epilogue == "relu2":
        acc = jnp.square(jnp.maximum(acc, 0.0))
    elif epilogue == "residual":
        acc = res_ref[...] + acc
    o_ref[...] = acc.astype(o_ref.dtype)


def _matmul_ws(a, w, layer, *, col0, n_out, out_dtype, tm, tn, epilogue="store", shift=0,
               transposed=False, res=None, name="matmul_ws"):
    m, kdim = a.shape
    blk0 = col0 // tn
    in_specs = [pl.BlockSpec((tm, kdim), lambda j, i: (i, 0))]
    if transposed:
        in_specs.append(pl.BlockSpec((None, tn, kdim), lambda j, i: (layer, blk0 + j, 0)))
    else:
        assert shift == 0
        in_specs.append(pl.BlockSpec((None, kdim, tn), lambda j, i: (layer, 0, blk0 + j)))
    args = [a, w]
    if shift:
        last_blk = (w.shape[1] - 1) // shift
        in_specs.append(pl.BlockSpec(
            (None, shift, kdim),
            lambda j, i: (layer, jnp.minimum((col0 + (j + 1) * tn) // shift, last_blk), 0)))
        args.append(w)
    if epilogue == "residual":
        in_specs.append(pl.BlockSpec((tm, tn), lambda j, i: (i, j)))
        args.append(res)
    wbf_shape = (tn, kdim) if transposed else (kdim, tn)
    return pl.pallas_call(
        functools.partial(_mm_ws_kernel, epilogue=epilogue, shift=shift, transposed=transposed),
        grid=(pl.cdiv(n_out, tn), m // tm),
        in_specs=in_specs,
        out_specs=pl.BlockSpec((tm, tn), lambda j, i: (i, j)),
        out_shape=jax.ShapeDtypeStruct((m, n_out), out_dtype),
        scratch_shapes=[pltpu.VMEM(wbf_shape, BF16)],
        compiler_params=_params(("parallel", "arbitrary")),
        name=name,
    )(*args)


def _sigmoid(x):
    return 1.0 / (1.0 + jnp.exp(-x))


def _softplus(x):
    return jnp.maximum(x, 0.0) + jnp.log(1.0 + jnp.exp(-jnp.abs(x)))


def _cumsum_rows(x):
    rows = x.shape[0]
    row = lax.broadcasted_iota(jnp.int32, x.shape, 0)
    s = 1
    while s < rows:
        x = x + jnp.where(row >= s, pltpu.roll(x, s, axis=0), 0.0)
        s *= 2
    return x


def _shift_rows(x, halo, s):
    xr = pltpu.roll(x, s, axis=0)
    hr = pltpu.roll(halo, s, axis=0)
    row = lax.broadcasted_iota(jnp.int32, halo.shape, 0)
    first = jnp.where(row < s, hr, xr[:SUBLANES])
    return jnp.concatenate([first, xr[SUBLANES:]], axis=0)


def _conv_silu(x, halo, w):
    acc = x * w[CONV_K - 1:CONV_K]
    for s in range(1, CONV_K):
        acc = acc + _shift_rows(x, halo, s) * w[CONV_K - 1 - s:CONV_K - s]
    return acc * _sigmoid(acc)


def _dot(a, b):
    return jnp.dot(a.astype(BF16), b.astype(BF16), preferred_element_type=F32)


def _dot_nt(a, b):
    return lax.dot_general(a.astype(BF16), b.astype(BF16), (((1,), (1,)), ((), ())),
                           preferred_element_type=F32)


def _dot_tn(a, b):
    return lax.dot_general(a.astype(BF16), b.astype(BF16), (((0,), (0,)), ((), ())),
                           preferred_element_type=F32)


def _split_bf16(x):
    hi = x.astype(BF16)
    return hi, (x - hi.astype(F32)).astype(BF16)


def _block_diag_pair(xp, lane_lo):
    zero = jnp.zeros_like(xp)
    return jnp.concatenate([jnp.where(lane_lo, xp, zero), jnp.where(lane_lo, zero, xp)], axis=0)


def _pair_dot3(l, xp, lane_lo):
    l_hi, l_lo = _split_bf16(l)
    x_hi, x_lo = _split_bf16(xp)
    bd_hi = _block_diag_pair(x_hi, lane_lo)
    bd_lo = _block_diag_pair(x_lo, lane_lo)
    lhs = jnp.concatenate([l_hi, l_lo, l_hi], axis=1)
    rhs = jnp.concatenate([bd_hi, bd_hi, bd_lo], axis=0)
    return jnp.dot(lhs, rhs, preferred_element_type=F32)


def _gdn_kernel(qkv_ref, z_ref, ba_ref, convw_ref, gp_ref, gnorm_ref, o_ref, state_ref, halo_ref):
    c = pl.program_id(1)

    @pl.when(c == 0)
    def _():
        state_ref[...] = jnp.zeros_like(state_ref)
        halo_ref[...] = jnp.zeros_like(halo_ref)

    ba = ba_ref[0]
    beta_all = _sigmoid(ba)
    g_all = -jnp.exp(gp_ref[0:1]) * _softplus(ba + gp_ref[1:2])
    decay_all = _cumsum_rows(g_all)
    decay_t2 = jnp.transpose(jnp.concatenate([decay_all, decay_all], axis=0))

    ri = lax.broadcasted_iota(jnp.int32, (CHUNK, 2 * CHUNK), 0)
    ci = lax.broadcasted_iota(jnp.int32, (CHUNK, 2 * CHUNK), 1)
    lane_lo = ci < CHUNK
    cj = jnp.where(lane_lo, ci, ci - CHUNK)
    causal = ri >= cj
    strict = ri > cj
    lane_lo_row = lane_lo[0:1]
    gnorm = gnorm_ref[...]
    q_scale = GDN_HEAD_DIM ** -0.5
    zeros_hd = jnp.zeros((CHUNK, GDN_HEAD_DIM), F32)
    zeros_rhs = jnp.zeros((CHUNK, 2 * GDN_HEAD_DIM), F32)

    def head_inputs(h):
        lo = h * GDN_HEAD_DIM
        sl_q = slice(lo, lo + GDN_HEAD_DIM)
        sl_k = slice(GDN_WIDTH + lo, GDN_WIDTH + lo + GDN_HEAD_DIM)
        sl_v = slice(2 * GDN_WIDTH + lo, 2 * GDN_WIDTH + lo + GDN_HEAD_DIM)
        qc = _conv_silu(qkv_ref[0, :, sl_q], halo_ref[:, sl_q], convw_ref[:, sl_q])
        kc = _conv_silu(qkv_ref[0, :, sl_k], halo_ref[:, sl_k], convw_ref[:, sl_k])
        vc = _conv_silu(qkv_ref[0, :, sl_v], halo_ref[:, sl_v], convw_ref[:, sl_v])
        qn = qc * (lax.rsqrt(jnp.sum(qc * qc, axis=-1, keepdims=True) + NORM_EPS) * q_scale)
        kn = kc * lax.rsqrt(jnp.sum(kc * kc, axis=-1, keepdims=True) + NORM_EPS)
        beta = beta_all[:, h:h + 1]
        dcol = decay_all[:, GDN_HEADS + h:GDN_HEADS + h + 1]
        dlast = dcol[CHUNK - 1:CHUNK]
        edec = jnp.exp(dcol)
        kb = kn * beta
        return dict(sl=sl_q, qn=qn, kn=kn, kb=kb, dcol=dcol, q_dec=qn * edec,
                    k_dec=kn * jnp.exp(dlast - dcol), sdec=jnp.exp(dlast),
                    rhs=jnp.concatenate([vc * beta, kb * edec], axis=1))

    for g0 in range(0, GDN_HEADS, GDN_HEAD_GROUP):
        pairs = [(h, h + 1) for h in range(g0, g0 + GDN_HEAD_GROUP, 2)]
        pre = [(head_inputs(h0), head_inputs(h1)) for h0, h1 in pairs]

        gamma, r = [], []
        for (h0, h1), (p0, p1) in zip(pairs, pre):
            drow = jnp.where(lane_lo_row, decay_t2[GDN_HEADS + h0:GDN_HEADS + h0 + 1],
                             decay_t2[GDN_HEADS + h1:GDN_HEADS + h1 + 1])
            dcol = jnp.where(lane_lo, p0["dcol"], p1["dcol"])
            gamma.append(jnp.exp(jnp.where(causal, dcol - drow, -jnp.inf)))
            lhs = jnp.concatenate([jnp.concatenate([p0["kb"], p1["kb"]], axis=1),
                                   jnp.concatenate([p0["qn"], p1["qn"]], axis=1)], axis=0)
            k_bd = jnp.concatenate([jnp.concatenate([p0["kn"], zeros_hd], axis=1),
                                    jnp.concatenate([zeros_hd, p1["kn"]], axis=1)], axis=0)
            r.append(_dot_nt(lhs, k_bd))
        a_low = [jnp.where(strict, ri_[:CHUNK] * gm, 0.0) for ri_, gm in zip(r, gamma)]
        qk = [ri_[CHUNK:] * gm for ri_, gm in zip(r, gamma)]

        nmat = [-a for a in a_low]
        xpow = [_pair_dot3(a, a, lane_lo) for a in a_low]
        p2 = 2
        while 2 * p2 < CHUNK:
            prod = [_pair_dot3(jnp.concatenate([nm, xp], axis=0), xp, lane_lo)
                    for nm, xp in zip(nmat, xpow)]
            nmat = [nm + xp + pr[:CHUNK] for nm, xp, pr in zip(nmat, xpow, prod)]
            xpow = [pr[CHUNK:] for pr in prod]
            p2 *= 2
        prod = [_pair_dot3(nm, xp, lane_lo) for nm, xp in zip(nmat, xpow)]
        nmat = [nm + xp + pr for nm, xp, pr in zip(nmat, xpow, prod)]

        heads, hp, uw = [], [], []
        for (h0, h1), (p0, p1), nm in zip(pairs, pre, nmat):
            n0 = jnp.where(lane_lo, nm, 0.0)
            n1 = jnp.where(lane_lo, 0.0, nm)
            uw.append(p0["rhs"] + _dot(n0, jnp.concatenate([p0["rhs"], zeros_rhs], axis=0)))
            uw.append(p1["rhs"] + _dot(n1, jnp.concatenate([zeros_rhs, p1["rhs"]], axis=0)))
            heads += [h0, h1]
            hp += [p0, p1]
        states = [state_ref[h] for h in heads]
        ws_qs = [_dot(jnp.concatenate([uwi[:, GDN_HEAD_DIM:], p["q_dec"]], axis=0), st)
                 for uwi, p, st in zip(uw, hp, states)]
        v_new = [uwi[:, :GDN_HEAD_DIM] - wq[:CHUNK] for uwi, wq in zip(uw, ws_qs)]
        o_att = []
        for i, qki in enumerate(qk):
            v_bd = jnp.concatenate([jnp.concatenate([v_new[2 * i], zeros_hd], axis=1),
                                    jnp.concatenate([zeros_hd, v_new[2 * i + 1]], axis=1)], axis=0)
            oa = _dot(qki, v_bd)
            o_att += [oa[:, :GDN_HEAD_DIM], oa[:, GDN_HEAD_DIM:]]
        kv_upd = [_dot_tn(p["k_dec"], vn) for p, vn in zip(hp, v_new)]
        for h, p, st, wq, oa, kvu in zip(heads, hp, states, ws_qs, o_att, kv_upd):
            state_ref[h] = st * p["sdec"] + kvu
            o = wq[CHUNK:] + oa
            on = o * lax.rsqrt(jnp.mean(o * o, axis=-1, keepdims=True) + NORM_EPS) * gnorm
            zc = z_ref[0, :, p["sl"]]
            o_ref[0, :, p["sl"]] = (on * (zc * _sigmoid(zc))).astype(o_ref.dtype)

    halo_ref[...] = qkv_ref[0, CHUNK - SUBLANES:CHUNK, :]


def _gdn(qkvz, ba, conv_w, a_log, dt_bias, gnorm):
    b, t, _ = qkvz.shape
    gp = jnp.zeros((2, LANES), F32)
    gp = gp.at[0, GDN_HEADS:2 * GDN_HEADS].set(a_log).at[1, GDN_HEADS:2 * GDN_HEADS].set(dt_bias)
    qkv_w = 3 * GDN_WIDTH
    return pl.pallas_call(
        _gdn_kernel,
        grid=(b, t // CHUNK),
        in_specs=[
            pl.BlockSpec((1, CHUNK, qkv_w), lambda i, c: (i, c, 0)),
            pl.BlockSpec((1, CHUNK, GDN_WIDTH), lambda i, c: (i, c, qkv_w // GDN_WIDTH)),
            pl.BlockSpec((1, CHUNK, LANES), lambda i, c: (i, c, 0)),
            pl.BlockSpec((CONV_K, qkv_w), lambda i, c: (0, 0)),
            pl.BlockSpec((2, LANES), lambda i, c: (0, 0)),
            pl.BlockSpec((1, GDN_HEAD_DIM), lambda i, c: (0, 0)),
        ],
        out_specs=pl.BlockSpec((1, CHUNK, GDN_WIDTH), lambda i, c: (i, c, 0)),
        out_shape=jax.ShapeDtypeStruct((b, t, GDN_WIDTH), BF16),
        scratch_shapes=[pltpu.VMEM((GDN_HEADS, GDN_HEAD_DIM, GDN_HEAD_DIM), F32),
                        pltpu.VMEM((SUBLANES, qkv_w), F32)],
        compiler_params=_params(("parallel", "arbitrary")),
        name="gdn",
    )(qkvz, qkvz, ba, conv_w, gp, gnorm.reshape(1, GDN_HEAD_DIM))


def _swa_kernel(sink_ref, q_ref, kp_ref, kc_ref, vp_ref, vc_ref, o_ref):
    nb = pl.program_id(1)
    kj = lax.broadcasted_iota(jnp.int32, (WINDOW, WINDOW), 0)
    qi = lax.broadcasted_iota(jnp.int32, (WINDOW, WINDOW), 1)
    from_cur = kj <= qi
    dist = jnp.where(from_cur, qi - kj, qi - kj + WINDOW).astype(F32)
    dist = jnp.where(jnp.logical_and(nb == 0, jnp.logical_not(from_cur)), jnp.inf, dist)
    lane_lo = qi < SWA_HEAD_DIM
    q_scale = SWA_HEAD_DIM ** -0.5
    pair_w = 2 * SWA_HEAD_DIM

    def lo_hi(tile, kv):
        swapped = pltpu.roll(tile, SWA_HEAD_DIM, axis=1)
        zero = jnp.zeros_like(tile)
        if kv % 2 == 0:
            return jnp.where(lane_lo, tile, zero), jnp.where(lane_lo, zero, swapped)
        return jnp.where(lane_lo, swapped, zero), jnp.where(lane_lo, zero, tile)

    batches = []
    for kv in range(SWA_KV_HEADS):
        tile_sl = slice((kv // 2) * pair_w, (kv // 2 + 1) * pair_w)
        kc_lo, kc_hi = lo_hi(kc_ref[0, :, tile_sl], kv)
        kp_lo, kp_hi = lo_hi(kp_ref[0, :, tile_sl], kv)
        kc2 = jnp.concatenate([kc_lo, kc_hi], axis=0).astype(BF16)
        kp2 = jnp.concatenate([kp_lo, kp_hi], axis=0).astype(BF16)
        v2 = jnp.concatenate([vc_ref[0, :, tile_sl], vp_ref[0, :, tile_sl]], axis=0).astype(BF16)
        row0 = (kv % 2) * SWA_HEAD_DIM
        for pp in range(SWA_GROUP // 4):
            head0 = kv * SWA_GROUP + 4 * pp
            col = head0 * SWA_HEAD_DIM
            q2 = jnp.concatenate([q_ref[0, :, col:col + pair_w],
                                  q_ref[0, :, col + pair_w:col + 2 * pair_w]], axis=0)
            q2 = (q2 * q_scale).astype(BF16)
            s_cur = lax.dot_general(kc2, q2, (((1,), (1,)), ((), ())), preferred_element_type=F32)
            s_prev = lax.dot_general(kp2, q2, (((1,), (1,)), ((), ())), preferred_element_type=F32)
            batches.append((head0, col, v2, row0, s_cur, s_prev))

    pairs = []
    for head0, col, v2, row0, s_cur, s_prev in batches:
        for pr in range(2):
            cs = slice(pr * WINDOW, (pr + 1) * WINDOW)
            p_cur, p_prev, inv_l = [], [], []
            for half in range(2):
                head = head0 + 2 * pr + half
                rs = slice(half * WINDOW, (half + 1) * WINDOW)
                slope = 2.0 ** (-8.0 * (head + 1) / SWA_Q_HEADS)
                s = jnp.where(from_cur, s_cur[rs, cs], s_prev[rs, cs]) - slope * dist
                sink = sink_ref[0, head]
                m = jnp.maximum(jnp.max(s, axis=0, keepdims=True), sink)
                p = jnp.exp(s - m)
                inv_l.append(1.0 / (jnp.sum(p, axis=0, keepdims=True) + jnp.exp(sink - m)))
                pb = p.astype(BF16)
                zero = jnp.zeros_like(pb)
                p_cur.append(jnp.where(from_cur, pb, zero))
                p_prev.append(jnp.where(from_cur, zero, pb))
            probs = jnp.concatenate([jnp.concatenate(p_cur, axis=1),
                                     jnp.concatenate(p_prev, axis=1)], axis=0)
            pairs.append((col + pr * pair_w, v2, row0, probs, inv_l))

    outs = [lax.dot_general(v2, probs, (((0,), (0,)), ((), ())), preferred_element_type=F32)
            for _, v2, _, probs, _ in pairs]
    for (pcol, _, row0, _, inv_l), o_t in zip(pairs, outs):
        o_t = o_t[row0:row0 + SWA_HEAD_DIM]
        o_pair = jnp.concatenate([o_t[:, :WINDOW] * inv_l[0],
                                  o_t[:, WINDOW:] * inv_l[1]], axis=0)
        o_ref[0, :, pcol:pcol + pair_w] = jnp.transpose(o_pair).astype(o_ref.dtype)


def _swa(rest, sinks):
    b, t, _ = rest.shape
    kb = ROFF_KS // SWA_KV_WIDTH
    vb = ROFF_VS // SWA_KV_WIDTH
    prev = lambda n: jnp.maximum(n - 1, 0)
    return pl.pallas_call(
        _swa_kernel,
        grid=(b, t // WINDOW),
        in_specs=[
            pl.BlockSpec(memory_space=pltpu.SMEM),
            pl.BlockSpec((1, WINDOW, SWA_Q_WIDTH), lambda i, n: (i, n, ROFF_QS // SWA_Q_WIDTH)),
            pl.BlockSpec((1, WINDOW, SWA_KV_WIDTH), lambda i, n: (i, prev(n), kb)),
            pl.BlockSpec((1, WINDOW, SWA_KV_WIDTH), lambda i, n: (i, n, kb)),
            pl.BlockSpec((1, WINDOW, SWA_KV_WIDTH), lambda i, n: (i, prev(n), vb)),
            pl.BlockSpec((1, WINDOW, SWA_KV_WIDTH), lambda i, n: (i, n, vb)),
        ],
        out_specs=pl.BlockSpec((1, WINDOW, SWA_Q_WIDTH), lambda i, n: (i, n, 0)),
        out_shape=jax.ShapeDtypeStruct((b, t, SWA_Q_WIDTH), BF16),
        compiler_params=_params(("parallel", "parallel")),
        name="swa",
    )(sinks.reshape(1, SWA_Q_HEADS), rest, rest, rest, rest, rest)


def _merge_kernel(og_ref, os_ref, wg_ref, ws_ref, gg_ref, gs_ref, o_ref, wg_bf, ws_bf):
    @pl.when(pl.program_id(1) == 0)
    def _():
        for r in range(0, wg_bf.shape[0], W_CAST_ROWS):
            rows = slice(r, r + W_CAST_ROWS)
            wg_bf[rows, :] = wg_ref[rows, :].astype(BF16)
            ws_bf[rows, :] = ws_ref[rows, :].astype(BF16)

    yg = jnp.dot(og_ref[...], wg_bf[...], preferred_element_type=F32)
    ys = jnp.dot(os_ref[...], ws_bf[...], preferred_element_type=F32)
    o_ref[...] = (_sigmoid(gg_ref[...]) * yg + _sigmoid(gs_ref[...]) * ys).astype(o_ref.dtype)


def _merge(o_gdn, o_swa, w_gdn, w_swa, layer, rest2d, tm, tn=512):
    m = o_gdn.shape[0]
    gate_blk = ROFF_GATES // tn
    return pl.pallas_call(
        _merge_kernel,
        grid=(D_MODEL // tn, m // tm),
        in_specs=[
            pl.BlockSpec((tm, GDN_WIDTH), lambda j, i: (i, 0)),
            pl.BlockSpec((tm, SWA_Q_WIDTH), lambda j, i: (i, 0)),
            pl.BlockSpec((None, GDN_WIDTH, tn), lambda j, i: (layer, 0, j)),
            pl.BlockSpec((None, SWA_Q_WIDTH, tn), lambda j, i: (layer, 0, j)),
            pl.BlockSpec((tm, tn), lambda j, i: (i, gate_blk + j)),
            pl.BlockSpec((tm, tn), lambda j, i: (i, gate_blk + D_MODEL // tn + j)),
        ],
        out_specs=pl.BlockSpec((tm, tn), lambda j, i: (i, j)),
        out_shape=jax.ShapeDtypeStruct((m, D_MODEL), BF16),
        scratch_shapes=[pltpu.VMEM((GDN_WIDTH, tn), BF16), pltpu.VMEM((SWA_Q_WIDTH, tn), BF16)],
        compiler_params=_params(("parallel", "arbitrary")),
        name="merge",
    )(o_gdn, o_swa, w_gdn, w_swa, rest2d, rest2d)


def kernel(x, norm1_g, w_in, conv_w, a_log, dt_bias, gdn_norm_g, attn_sinks, w_branch_gdn,
           w_branch_swa, w_out, norm2_g, w_ff_up, w_ff_down, final_norm_g):
    b, t, d = x.shape
    depth = w_in.shape[0]
    n = b * t
    tm, tn = ROW_TILE, COL_TILE
    w_in_t = jnp.swapaxes(w_in, 1, 2)
    w_down = w_ff_down.astype(BF16)

    xf = x.reshape(n, d)
    for l in range(depth):
        h = _rmsnorm(xf, norm1_g[l], BF16)
        qkvz = _matmul_ws(h, w_in_t, l, col0=0, n_out=QKVZ_WIDTH, out_dtype=F32, tm=tm, tn=tn,
                          transposed=True, name="in_proj_qkvz")
        ba = _matmul_ws(h, w_in_t, l, col0=SRC_BA, n_out=LANES, out_dtype=F32, tm=tm, tn=LANES,
                        transposed=True, name="in_proj_ba")
        rest = _matmul_ws(h, w_in_t, l, col0=SRC_BA, n_out=REST_WIDTH, out_dtype=F32, tm=tm, tn=tn,
                          shift=REST_SHIFT, transposed=True, name="in_proj_rest")
        o_gdn = _gdn(qkvz.reshape(b, t, QKVZ_WIDTH), ba.reshape(b, t, LANES), conv_w[l], a_log[l],
                     dt_bias[l], gdn_norm_g[l]).reshape(n, GDN_WIDTH)
        o_swa = _swa(rest.reshape(b, t, REST_WIDTH), attn_sinks[l]).reshape(n, SWA_Q_WIDTH)
        merged = _merge(o_gdn, o_swa, w_branch_gdn, w_branch_swa, l, rest, tm=tm)
        xf = _matmul_ws(merged, w_out, l, col0=0, n_out=D_MODEL, out_dtype=F32, tm=tm, tn=tn,
                        epilogue="residual", res=xf, name="out_proj")
        h2 = _rmsnorm(xf, norm2_g[l], BF16)
        act = _matmul_ws(h2, w_ff_up, l, col0=0, n_out=D_FF, out_dtype=BF16, tm=tm, tn=tn,
                         epilogue="relu2", name="ff_up")
        xf = _matmul(act, w_down, l, out_dtype=F32, tm=tm, tn=tn, tk=2048, epilogue="residual",
                     res=xf, name="ff_down")
    return _rmsnorm(xf, final_norm_g, F32).reshape(b, t, d)
```

```python
import functools

import jax
import jax.numpy as jnp
from jax import lax
from jax.experimental import pallas as pl
from jax.experimental.pallas import tpu as pltpu

F32 = jnp.float32
BF16 = jnp.bfloat16

D_MODEL = 2048
GDN_HEAD_DIM = 128
GDN_HEADS = 16
GDN_WIDTH = GDN_HEADS * GDN_HEAD_DIM
CONV_K = 4
CHUNK = 64
GDN_HEAD_GROUP = 16
SWA_HEAD_DIM = 64
SWA_Q_HEADS = 32
SWA_KV_HEADS = 4
SWA_GROUP = SWA_Q_HEADS // SWA_KV_HEADS
SWA_Q_WIDTH = SWA_Q_HEADS * SWA_HEAD_DIM
SWA_KV_WIDTH = SWA_KV_HEADS * SWA_HEAD_DIM
WINDOW = 128
D_FF = 4 * D_MODEL
NORM_EPS = 1e-6

LANES = 128
SUBLANES = 8
VMEM_LIMIT_BYTES = 56 * 1024 * 1024

W_CAST_ROWS = 256
ROW_TILE = 1024
COL_TILE = 1024

QKVZ_WIDTH = 4 * GDN_WIDTH
SRC_BA = QKVZ_WIDTH
SRC_REST = SRC_BA + 2 * GDN_HEADS
REST_SHIFT = SRC_REST - SRC_BA
REST_WIDTH = SWA_Q_WIDTH + 2 * SWA_KV_WIDTH + 2 * D_MODEL
D_IN = SRC_REST + REST_WIDTH
ROFF_QS = 0
ROFF_KS = ROFF_QS + SWA_Q_WIDTH
ROFF_VS = ROFF_KS + SWA_KV_WIDTH
ROFF_GATES = ROFF_VS + SWA_KV_WIDTH


def _params(semantics):
    return pltpu.CompilerParams(dimension_semantics=semantics, vmem_limit_bytes=VMEM_LIMIT_BYTES)


def _rmsnorm_kernel(x_ref, g_ref, o_ref):
    x = x_ref[...]
    ms = jnp.mean(x * x, axis=-1, keepdims=True)
    o_ref[...] = (x * lax.rsqrt(ms + NORM_EPS) * g_ref[...]).astype(o_ref.dtype)


def _rmsnorm(x, gain, out_dtype, rows=512):
    n, d = x.shape
    return pl.pallas_call(
        _rmsnorm_kernel,
        grid=(n // rows,),
        in_specs=[pl.BlockSpec((rows, d), lambda i: (i, 0)),
                  pl.BlockSpec((1, d), lambda i: (0, 0))],
        out_specs=pl.BlockSpec((rows, d), lambda i: (i, 0)),
        out_shape=jax.ShapeDtypeStruct((n, d), out_dtype),
        compiler_params=_params(("parallel",)),
        name="rmsnorm",
    )(x, gain.reshape(1, d))


def _mm_kernel(*refs, nk, epilogue):
    if epilogue == "residual":
        a_ref, w_ref, r_ref, o_ref = refs[:4]
        rest = refs[4:]
    else:
        a_ref, w_ref, o_ref = refs[:3]
        r_ref = None
        rest = refs[3:]

    def finish(acc):
        if epilogue == "relu2":
            acc = jnp.square(jnp.maximum(acc, 0.0))
        elif epilogue == "residual":
            acc = r_ref[...] + acc
        o_ref[...] = acc.astype(o_ref.dtype)

    part = jnp.dot(a_ref[...], w_ref[...], preferred_element_type=F32)
    if nk == 1:
        finish(part)
        return
    acc_ref = rest[0]
    k = pl.program_id(2)

    @pl.when(k == 0)
    def _():
        acc_ref[...] = part

    @pl.when(jnp.logical_and(k > 0, k < nk - 1))
    def _():
        acc_ref[...] += part

    @pl.when(k == nk - 1)
    def _():
        finish(acc_ref[...] + part)


def _matmul(a, w, layer, *, out_dtype, tm, tn, tk=None, epilogue="store", res=None, name="matmul"):
    m, kdim = a.shape
    n = w.shape[2]
    tk = kdim if tk is None else tk
    nk = kdim // tk
    grid = (m // tm, pl.cdiv(n, tn), nk)
    in_specs = [pl.BlockSpec((tm, tk), lambda i, j, k: (i, k)),
                pl.BlockSpec((None, tk, tn), lambda i, j, k: (layer, k, j))]
    args = [a, w]
    if epilogue == "residual":
        in_specs.append(pl.BlockSpec((tm, tn), lambda i, j, k: (i, j)))
        args.append(res)
    scratch = [pltpu.VMEM((tm, tn), F32)] if nk > 1 else []
    return pl.pallas_call(
        functools.partial(_mm_kernel, nk=nk, epilogue=epilogue),
        grid=grid,
        in_specs=in_specs,
        out_specs=pl.BlockSpec((tm, tn), lambda i, j, k: (i, j)),
        out_shape=jax.ShapeDtypeStruct((m, n), out_dtype),
        scratch_shapes=scratch,
        compiler_params=_params(("parallel", "parallel", "arbitrary")),
        name=name,
    )(*args)


def _mm_ws_kernel(*refs, epilogue, shift, transposed):
    refs = list(refs)
    a_ref, w_ref = refs[:2]
    pos = 2
    trail_ref = res_ref = None
    if shift:
        trail_ref = refs[pos]
        pos += 1
    if epilogue == "residual":
        res_ref = refs[pos]
        pos += 1
    o_ref, wbf_ref = refs[pos], refs[pos + 1]
    i = pl.program_id(1)
    n_rows = wbf_ref.shape[0]

    @pl.when(i == 0)
    def _():
        body = n_rows - shift
        for r in range(0, body, W_CAST_ROWS):
            rows = min(W_CAST_ROWS, body - r)
            wbf_ref[r:r + rows, :] = w_ref[r + shift:r + shift + rows, :].astype(BF16)
        if shift:
            wbf_ref[body:, :] = trail_ref[...].astype(BF16)

    if transposed:
        acc = lax.dot_general(a_ref[...], wbf_ref[...], (((1,), (1,)), ((), ())),
                              preferred_element_type=F32)
    else:
        acc = jnp.dot(a_ref[...], wbf_ref[...], preferred_element_type=F32)
    if epilogue == "relu2":
        acc = jnp.square(jnp.maximum(acc, 0.0))
    elif epilogue == "residual":
        acc = res_ref[...] + acc
    o_ref[...] = acc.astype(o_ref.dtype)


def _matmul_ws(a, w, layer, *, col0, n_out, out_dtype, tm, tn, epilogue="store", shift=0,
               transposed=False, res=None, name="matmul_ws"):
    m, kdim = a.shape
    blk0 = col0 // tn
    in_specs = [pl.BlockSpec((tm, kdim), lambda j, i: (i, 0))]
    if transposed:
        in_specs.append(pl.BlockSpec((None, tn, kdim), lambda j, i: (layer, blk0 + j, 0)))
    else:
        assert shift == 0
        in_specs.append(pl.BlockSpec((None, kdim, tn), lambda j, i: (layer, 0, blk0 + j)))
    args = [a, w]
    if shift:
        last_blk = (w.shape[1] - 1) // shift
        in_specs.append(pl.BlockSpec(
            (None, shift, kdim),
            lambda j, i: (layer, jnp.minimum((col0 + (j + 1) * tn) // shift, last_blk), 0)))
        args.append(w)
    if epilogue == "residual":
        in_specs.append(pl.BlockSpec((tm, tn), lambda j, i: (i, j)))
        args.append(res)
    wbf_shape = (tn, kdim) if transposed else (kdim, tn)
    return pl.pallas_call(
        functools.partial(_mm_ws_kernel, epilogue=epilogue, shift=shift, transposed=transposed),
        grid=(pl.cdiv(n_out, tn), m // tm),
        in_specs=in_specs,
        out_specs=pl.BlockSpec((tm, tn), lambda j, i: (i, j)),
        out_shape=jax.ShapeDtypeStruct((m, n_out), out_dtype),
        scratch_shapes=[pltpu.VMEM(wbf_shape, BF16)],
        compiler_params=_params(("parallel", "arbitrary")),
        name=name,
    )(*args)


def _sigmoid(x):
    return 0.5 * jnp.tanh(0.5 * x) + 0.5


def _silu(x):
    u = 0.5 * x
    return u * jnp.tanh(u) + u


def _softplus(x):
    return jnp.maximum(x, 0.0) + jnp.log(1.0 + jnp.exp(-jnp.abs(x)))


def _cumsum_rows(x):
    rows = x.shape[0]
    row = lax.broadcasted_iota(jnp.int32, x.shape, 0)
    s = 1
    while s < rows:
        x = x + jnp.where(row >= s, pltpu.roll(x, s, axis=0), 0.0)
        s *= 2
    return x


def _shift_rows(x, halo, s):
    xr = pltpu.roll(x, s, axis=0)
    hr = pltpu.roll(halo, s, axis=0)
    row = lax.broadcasted_iota(jnp.int32, halo.shape, 0)
    first = jnp.where(row < s, hr, xr[:SUBLANES])
    return jnp.concatenate([first, xr[SUBLANES:]], axis=0)


def _conv_silu(x, halo, w):
    acc = x * w[CONV_K - 1:CONV_K]
    for s in range(1, CONV_K):
        acc = acc + _shift_rows(x, halo, s) * w[CONV_K - 1 - s:CONV_K - s]
    return _silu(acc)


def _dot(a, b):
    return jnp.dot(a.astype(BF16), b.astype(BF16), preferred_element_type=F32)


def _dot_nt(a, b):
    return lax.dot_general(a.astype(BF16), b.astype(BF16), (((1,), (1,)), ((), ())),
                           preferred_element_type=F32)


def _dot_tn(a, b):
    return lax.dot_general(a.astype(BF16), b.astype(BF16), (((0,), (0,)), ((), ())),
                           preferred_element_type=F32)


def _split_bf16(x):
    hi = x.astype(BF16)
    return hi, (x - hi.astype(F32)).astype(BF16)


def _block_diag_pair(xp, lane_lo):
    zero = jnp.zeros_like(xp)
    return jnp.concatenate([jnp.where(lane_lo, xp, zero), jnp.where(lane_lo, zero, xp)], axis=0)


def _pair_dot3(l_parts, x_parts, lane_lo):
    l_hi, l_lo = l_parts
    x_hi, x_lo = x_parts
    bd_hi = _block_diag_pair(x_hi, lane_lo)
    bd_lo = _block_diag_pair(x_lo, lane_lo)
    lhs = jnp.concatenate([l_hi, l_lo, l_hi], axis=1)
    rhs = jnp.concatenate([bd_hi, bd_hi, bd_lo], axis=0)
    return jnp.dot(lhs, rhs, preferred_element_type=F32)


def _gdn_kernel(qkv_ref, z_ref, ba_ref, convw_ref, gp_ref, gnorm_ref, o_ref, state_ref, halo_ref):
    c = pl.program_id(1)

    @pl.when(c == 0)
    def _():
        state_ref[...] = jnp.zeros_like(state_ref)
        halo_ref[...] = jnp.zeros_like(halo_ref)

    ba = ba_ref[0]
    beta_all = _sigmoid(ba)
    g_all = -jnp.exp(gp_ref[0:1]) * _softplus(ba + gp_ref[1:2])
    decay_all = _cumsum_rows(g_all)
    decay_t2 = jnp.transpose(jnp.concatenate([decay_all, decay_all], axis=0))

    ri = lax.broadcasted_iota(jnp.int32, (CHUNK, 2 * CHUNK), 0)
    ci = lax.broadcasted_iota(jnp.int32, (CHUNK, 2 * CHUNK), 1)
    lane_lo = ci < CHUNK
    cj = jnp.where(lane_lo, ci, ci - CHUNK)
    causal = ri >= cj
    strict = ri > cj
    lane_lo_row = lane_lo[0:1]
    gnorm = gnorm_ref[...]
    q_scale = GDN_HEAD_DIM ** -0.5
    zeros_hd = jnp.zeros((CHUNK, GDN_HEAD_DIM), F32)
    zeros_rhs = jnp.zeros((CHUNK, 2 * GDN_HEAD_DIM), F32)

    def head_inputs(h):
        lo = h * GDN_HEAD_DIM
        sl_q = slice(lo, lo + GDN_HEAD_DIM)
        sl_k = slice(GDN_WIDTH + lo, GDN_WIDTH + lo + GDN_HEAD_DIM)
        sl_v = slice(2 * GDN_WIDTH + lo, 2 * GDN_WIDTH + lo + GDN_HEAD_DIM)
        qc = _conv_silu(qkv_ref[0, :, sl_q], halo_ref[:, sl_q], convw_ref[:, sl_q])
        kc = _conv_silu(qkv_ref[0, :, sl_k], halo_ref[:, sl_k], convw_ref[:, sl_k])
        vc = _conv_silu(qkv_ref[0, :, sl_v], halo_ref[:, sl_v], convw_ref[:, sl_v])
        qn = qc * (lax.rsqrt(jnp.sum(qc * qc, axis=-1, keepdims=True) + NORM_EPS) * q_scale)
        kn = kc * lax.rsqrt(jnp.sum(kc * kc, axis=-1, keepdims=True) + NORM_EPS)
        beta = beta_all[:, h:h + 1]
        dcol = decay_all[:, GDN_HEADS + h:GDN_HEADS + h + 1]
        dlast = dcol[CHUNK - 1:CHUNK]
        edec = jnp.exp(dcol)
        kb = kn * beta
        return dict(sl=sl_q, qn=qn, kn=kn, kb=kb, dcol=dcol, q_dec=qn * edec,
                    k_dec=kn * jnp.exp(dlast - dcol), sdec=jnp.exp(dlast),
                    rhs=jnp.concatenate([vc * beta, kb * edec], axis=1))

    for g0 in range(0, GDN_HEADS, GDN_HEAD_GROUP):
        pairs = [(h, h + 1) for h in range(g0, g0 + GDN_HEAD_GROUP, 2)]
        pre = [(head_inputs(h0), head_inputs(h1)) for h0, h1 in pairs]

        gamma, r = [], []
        for (h0, h1), (p0, p1) in zip(pairs, pre):
            drow = jnp.where(lane_lo_row, decay_t2[GDN_HEADS + h0:GDN_HEADS + h0 + 1],
                             decay_t2[GDN_HEADS + h1:GDN_HEADS + h1 + 1])
            dcol = jnp.where(lane_lo, p0["dcol"], p1["dcol"])
            gamma.append(jnp.exp(jnp.where(causal, dcol - drow, -jnp.inf)))
            lhs = jnp.concatenate([jnp.concatenate([p0["kb"], p1["kb"]], axis=1),
                                   jnp.concatenate([p0["qn"], p1["qn"]], axis=1)], axis=0)
            k_bd = jnp.concatenate([jnp.concatenate([p0["kn"], zeros_hd], axis=1),
                                    jnp.concatenate([zeros_hd, p1["kn"]], axis=1)], axis=0)
            r.append(_dot_nt(lhs, k_bd))
        a_low = [jnp.where(strict, ri_[:CHUNK] * gm, 0.0) for ri_, gm in zip(r, gamma)]
        qk = [ri_[CHUNK:] * gm for ri_, gm in zip(r, gamma)]

        nmat = [-a for a in a_low]
        a_parts = [_split_bf16(a) for a in a_low]
        xpow = [_pair_dot3(ap, ap, lane_lo) for ap in a_parts]
        p2 = 2
        while 2 * p2 < CHUNK:
            n_parts = [_split_bf16(nm) for nm in nmat]
            x_parts = [_split_bf16(xp) for xp in xpow]
            prod = [_pair_dot3(tuple(jnp.concatenate([n_, x_], axis=0) for n_, x_ in zip(npt, xpt)),
                               xpt, lane_lo) for npt, xpt in zip(n_parts, x_parts)]
            nmat = [nm + xp + pr[:CHUNK] for nm, xp, pr in zip(nmat, xpow, prod)]
            xpow = [pr[CHUNK:] for pr in prod]
            p2 *= 2
        prod = [_pair_dot3(_split_bf16(nm), _split_bf16(xp), lane_lo) for nm, xp in zip(nmat, xpow)]
        nmat = [nm + xp + pr for nm, xp, pr in zip(nmat, xpow, prod)]

        heads, hp, uw = [], [], []
        for (h0, h1), (p0, p1), nm in zip(pairs, pre, nmat):
            n0 = jnp.where(lane_lo, nm, 0.0)
            n1 = jnp.where(lane_lo, 0.0, nm)
            uw.append(p0["rhs"] + _dot(n0, jnp.concatenate([p0["rhs"], zeros_rhs], axis=0)))
            uw.append(p1["rhs"] + _dot(n1, jnp.concatenate([zeros_rhs, p1["rhs"]], axis=0)))
            heads += [h0, h1]
            hp += [p0, p1]
        states = [state_ref[h] for h in heads]
        ws_qs = [_dot(jnp.concatenate([uwi[:, GDN_HEAD_DIM:], p["q_dec"]], axis=0), st)
                 for uwi, p, st in zip(uw, hp, states)]
        v_new = [uwi[:, :GDN_HEAD_DIM] - wq[:CHUNK] for uwi, wq in zip(uw, ws_qs)]
        o_att = []
        for i, qki in enumerate(qk):
            v_bd = jnp.concatenate([jnp.concatenate([v_new[2 * i], zeros_hd], axis=1),
                                    jnp.concatenate([zeros_hd, v_new[2 * i + 1]], axis=1)], axis=0)
            oa = _dot(qki, v_bd)
            o_att += [oa[:, :GDN_HEAD_DIM], oa[:, GDN_HEAD_DIM:]]
        kv_upd = [_dot_tn(p["k_dec"], vn) for p, vn in zip(hp, v_new)]
        for h, p, st, wq, oa, kvu in zip(heads, hp, states, ws_qs, o_att, kv_upd):
            state_ref[h] = st * p["sdec"] + kvu
            o = wq[CHUNK:] + oa
            on = o * lax.rsqrt(jnp.mean(o * o, axis=-1, keepdims=True) + NORM_EPS) * gnorm
            zc = z_ref[0, :, p["sl"]]
            o_ref[0, :, p["sl"]] = (on * _silu(zc)).astype(o_ref.dtype)

    halo_ref[...] = qkv_ref[0, CHUNK - SUBLANES:CHUNK, :]


def _gdn(qkvz, ba, conv_w, a_log, dt_bias, gnorm):
    b, t, _ = qkvz.shape
    gp = jnp.zeros((2, LANES), F32)
    gp = gp.at[0, GDN_HEADS:2 * GDN_HEADS].set(a_log).at[1, GDN_HEADS:2 * GDN_HEADS].set(dt_bias)
    qkv_w = 3 * GDN_WIDTH
    return pl.pallas_call(
        _gdn_kernel,
        grid=(b, t // CHUNK),
        in_specs=[
            pl.BlockSpec((1, CHUNK, qkv_w), lambda i, c: (i, c, 0)),
            pl.BlockSpec((1, CHUNK, GDN_WIDTH), lambda i, c: (i, c, qkv_w // GDN_WIDTH)),
            pl.BlockSpec((1, CHUNK, LANES), lambda i, c: (i, c, 0)),
            pl.BlockSpec((CONV_K, qkv_w), lambda i, c: (0, 0)),
            pl.BlockSpec((2, LANES), lambda i, c: (0, 0)),
            pl.BlockSpec((1, GDN_HEAD_DIM), lambda i, c: (0, 0)),
        ],
        out_specs=pl.BlockSpec((1, CHUNK, GDN_WIDTH), lambda i, c: (i, c, 0)),
        out_shape=jax.ShapeDtypeStruct((b, t, GDN_WIDTH), BF16),
        scratch_shapes=[pltpu.VMEM((GDN_HEADS, GDN_HEAD_DIM, GDN_HEAD_DIM), F32),
                        pltpu.VMEM((SUBLANES, qkv_w), F32)],
        compiler_params=_params(("parallel", "arbitrary")),
        name="gdn",
    )(qkvz, qkvz, ba, conv_w, gp, gnorm.reshape(1, GDN_HEAD_DIM))


def _swa_kernel(sink_ref, q_ref, kp_ref, kc_ref, vp_ref, vc_ref, o_ref):
    nb = pl.program_id(1)
    kj = lax.broadcasted_iota(jnp.int32, (WINDOW, WINDOW), 0)
    qi = lax.broadcasted_iota(jnp.int32, (WINDOW, WINDOW), 1)
    from_cur = kj <= qi
    dist = jnp.where(from_cur, qi - kj, qi - kj + WINDOW).astype(F32)
    dist = jnp.where(jnp.logical_and(nb == 0, jnp.logical_not(from_cur)), jnp.inf, dist)
    lane_lo = qi < SWA_HEAD_DIM
    q_scale = SWA_HEAD_DIM ** -0.5
    pair_w = 2 * SWA_HEAD_DIM

    def lo_hi(tile, kv):
        swapped = pltpu.roll(tile, SWA_HEAD_DIM, axis=1)
        zero = jnp.zeros_like(tile)
        if kv % 2 == 0:
            return jnp.where(lane_lo, tile, zero), jnp.where(lane_lo, zero, swapped)
        return jnp.where(lane_lo, swapped, zero), jnp.where(lane_lo, zero, tile)

    batches = []
    for kv in range(SWA_KV_HEADS):
        tile_sl = slice((kv // 2) * pair_w, (kv // 2 + 1) * pair_w)
        kc_lo, kc_hi = lo_hi(kc_ref[0, :, tile_sl], kv)
        kp_lo, kp_hi = lo_hi(kp_ref[0, :, tile_sl], kv)
        kc2 = jnp.concatenate([kc_lo, kc_hi], axis=0).astype(BF16)
        kp2 = jnp.concatenate([kp_lo, kp_hi], axis=0).astype(BF16)
        v2 = jnp.concatenate([vc_ref[0, :, tile_sl], vp_ref[0, :, tile_sl]], axis=0).astype(BF16)
        row0 = (kv % 2) * SWA_HEAD_DIM
        for pp in range(SWA_GROUP // 4):
            head0 = kv * SWA_GROUP + 4 * pp
            col = head0 * SWA_HEAD_DIM
            q2 = jnp.concatenate([q_ref[0, :, col:col + pair_w],
                                  q_ref[0, :, col + pair_w:col + 2 * pair_w]], axis=0)
            q2 = (q2 * q_scale).astype(BF16)
            s_cur = lax.dot_general(kc2, q2, (((1,), (1,)), ((), ())), preferred_element_type=F32)
            s_prev = lax.dot_general(kp2, q2, (((1,), (1,)), ((), ())), preferred_element_type=F32)
            batches.append((head0, col, v2, row0, s_cur, s_prev))

    pairs = []
    for head0, col, v2, row0, s_cur, s_prev in batches:
        for pr in range(2):
            cs = slice(pr * WINDOW, (pr + 1) * WINDOW)
            p_cur, p_prev, inv_l = [], [], []
            for half in range(2):
                head = head0 + 2 * pr + half
                rs = slice(half * WINDOW, (half + 1) * WINDOW)
                slope = 2.0 ** (-8.0 * (head + 1) / SWA_Q_HEADS)
                s = jnp.where(from_cur, s_cur[rs, cs], s_prev[rs, cs]) - slope * dist
                sink = sink_ref[0, head]
                m = jnp.maximum(jnp.max(s, axis=0, keepdims=True), sink)
                p = jnp.exp(s - m)
                inv_l.append(1.0 / (jnp.sum(p, axis=0, keepdims=True) + jnp.exp(sink - m)))
                pb = p.astype(BF16)
                zero = jnp.zeros_like(pb)
                p_cur.append(jnp.where(from_cur, pb, zero))
                p_prev.append(jnp.where(from_cur, zero, pb))
            probs = jnp.concatenate([jnp.concatenate(p_cur, axis=1),
                                     jnp.concatenate(p_prev, axis=1)], axis=0)
            pairs.append((col + pr * pair_w, v2, row0, probs, inv_l))

    outs = [lax.dot_general(v2, probs, (((0,), (0,)), ((), ())), preferred_element_type=F32)
            for _, v2, _, probs, _ in pairs]
    for (pcol, _, row0, _, inv_l), o_t in zip(pairs, outs):
        o_t = o_t[row0:row0 + SWA_HEAD_DIM]
        o_pair = jnp.concatenate([o_t[:, :WINDOW] * inv_l[0],
                                  o_t[:, WINDOW:] * inv_l[1]], axis=0)
        o_ref[0, :, pcol:pcol + pair_w] = jnp.transpose(o_pair).astype(o_ref.dtype)


def _swa(rest, sinks):
    b, t, _ = rest.shape
    kb = ROFF_KS // SWA_KV_WIDTH
    vb = ROFF_VS // SWA_KV_WIDTH
    prev = lambda n: jnp.maximum(n - 1, 0)
    return pl.pallas_call(
        _swa_kernel,
        grid=(b, t // WINDOW),
        in_specs=[
            pl.BlockSpec(memory_space=pltpu.SMEM),
            pl.BlockSpec((1, WINDOW, SWA_Q_WIDTH), lambda i, n: (i, n, ROFF_QS // SWA_Q_WIDTH)),
            pl.BlockSpec((1, WINDOW, SWA_KV_WIDTH), lambda i, n: (i, prev(n), kb)),
            pl.BlockSpec((1, WINDOW, SWA_KV_WIDTH), lambda i, n: (i, n, kb)),
            pl.BlockSpec((1, WINDOW, SWA_KV_WIDTH), lambda i, n: (i, prev(n), vb)),
            pl.BlockSpec((1, WINDOW, SWA_KV_WIDTH), lambda i, n: (i, n, vb)),
        ],
        out_specs=pl.BlockSpec((1, WINDOW, SWA_Q_WIDTH), lambda i, n: (i, n, 0)),
        out_shape=jax.ShapeDtypeStruct((b, t, SWA_Q_WIDTH), BF16),
        compiler_params=_params(("parallel", "parallel")),
        name="swa",
    )(sinks.reshape(1, SWA_Q_HEADS), rest, rest, rest, rest, rest)


def _merge_kernel(og_ref, os_ref, wg_ref, ws_ref, gg_ref, gs_ref, o_ref, wg_bf, ws_bf):
    @pl.when(pl.program_id(1) == 0)
    def _():
        for r in range(0, wg_bf.shape[0], W_CAST_ROWS):
            rows = slice(r, r + W_CAST_ROWS)
            wg_bf[rows, :] = wg_ref[rows, :].astype(BF16)
            ws_bf[rows, :] = ws_ref[rows, :].astype(BF16)

    yg = jnp.dot(og_ref[...], wg_bf[...], preferred_element_type=F32)
    ys = jnp.dot(os_ref[...], ws_bf[...], preferred_element_type=F32)
    o_ref[...] = (_sigmoid(gg_ref[...]) * yg + _sigmoid(gs_ref[...]) * ys).astype(o_ref.dtype)


def _merge(o_gdn, o_swa, w_gdn, w_swa, layer, rest2d, tm, tn=512):
    m = o_gdn.shape[0]
    gate_blk = ROFF_GATES // tn
    return pl.pallas_call(
        _merge_kernel,
        grid=(D_MODEL // tn, m // tm),
        in_specs=[
            pl.BlockSpec((tm, GDN_WIDTH), lambda j, i: (i, 0)),
            pl.BlockSpec((tm, SWA_Q_WIDTH), lambda j, i: (i, 0)),
            pl.BlockSpec((None, GDN_WIDTH, tn), lambda j, i: (layer, 0, j)),
            pl.BlockSpec((None, SWA_Q_WIDTH, tn), lambda j, i: (layer, 0, j)),
            pl.BlockSpec((tm, tn), lambda j, i: (i, gate_blk + j)),
            pl.BlockSpec((tm, tn), lambda j, i: (i, gate_blk + D_MODEL // tn + j)),
        ],
        out_specs=pl.BlockSpec((tm, tn), lambda j, i: (i, j)),
        out_shape=jax.ShapeDtypeStruct((m, D_MODEL), BF16),
        scratch_shapes=[pltpu.VMEM((GDN_WIDTH, tn), BF16), pltpu.VMEM((SWA_Q_WIDTH, tn), BF16)],
        compiler_params=_params(("parallel", "arbitrary")),
        name="merge",
    )(o_gdn, o_swa, w_gdn, w_swa, rest2d, rest2d)


def kernel(x, norm1_g, w_in, conv_w, a_log, dt_bias, gdn_norm_g, attn_sinks, w_branch_gdn,
           w_branch_swa, w_out, norm2_g, w_ff_up, w_ff_down, final_norm_g):
    b, t, d = x.shape
    depth = w_in.shape[0]
    n = b * t
    tm, tn = ROW_TILE, COL_TILE
    w_in_t = jnp.swapaxes(w_in, 1, 2)
    w_down = w_ff_down.astype(BF16)

    xf = x.reshape(n, d)
    for l in range(depth):
        h = _rmsnorm(xf, norm1_g[l], BF16)
        qkvz = _matmul_ws(h, w_in_t, l, col0=0, n_out=QKVZ_WIDTH, out_dtype=F32, tm=tm, tn=tn,
                          transposed=True, name="in_proj_qkvz")
        ba = _matmul_ws(h, w_in_t, l, col0=SRC_BA, n_out=LANES, out_dtype=F32, tm=tm, tn=LANES,
                        transposed=True, name="in_proj_ba")
        rest = _matmul_ws(h, w_in_t, l, col0=SRC_BA, n_out=REST_WIDTH, out_dtype=F32, tm=tm, tn=tn,
                          shift=REST_SHIFT, transposed=True, name="in_proj_rest")
        o_gdn = _gdn(qkvz.reshape(b, t, QKVZ_WIDTH), ba.reshape(b, t, LANES), conv_w[l], a_log[l],
                     dt_bias[l], gdn_norm_g[l]).reshape(n, GDN_WIDTH)
        o_swa = _swa(rest.reshape(b, t, REST_WIDTH), attn_sinks[l]).reshape(n, SWA_Q_WIDTH)
        merged = _merge(o_gdn, o_swa, w_branch_gdn, w_branch_swa, l, rest, tm=tm)
        xf = _matmul_ws(merged, w_out, l, col0=0, n_out=D_MODEL, out_dtype=F32, tm=tm, tn=tn,
                        epilogue="residual", res=xf, name="out_proj")
        h2 = _rmsnorm(xf, norm2_g[l], BF16)
        act = _matmul_ws(h2, w_ff_up, l, col0=0, n_out=D_FF, out_dtype=BF16, tm=tm, tn=tn,
                         epilogue="relu2", name="ff_up")
        xf = _matmul(act, w_down, l, out_dtype=F32, tm=tm // 2, tn=tn // 2, epilogue="residual",
                     res=xf, name="ff_down")
    return _rmsnorm(xf, final_norm_g, F32).reshape(b, t, d)
```

```python
import functools

import jax
import jax.numpy as jnp
from jax import lax
from jax.experimental import pallas as pl
from jax.experimental.pallas import tpu as pltpu

F32 = jnp.float32
BF16 = jnp.bfloat16

D_MODEL = 2048
GDN_HEAD_DIM = 128
GDN_HEADS = 16
GDN_WIDTH = GDN_HEADS * GDN_HEAD_DIM
CONV_K = 4
CHUNK = 64
GDN_HEAD_GROUP = 16
GDN_CHUNKS_PER_STEP = 2
SWA_HEAD_DIM = 64
SWA_Q_HEADS = 32
SWA_KV_HEADS = 4
SWA_GROUP = SWA_Q_HEADS // SWA_KV_HEADS
SWA_Q_WIDTH = SWA_Q_HEADS * SWA_HEAD_DIM
SWA_KV_WIDTH = SWA_KV_HEADS * SWA_HEAD_DIM
WINDOW = 128
D_FF = 4 * D_MODEL
NORM_EPS = 1e-6

LANES = 128
SUBLANES = 8
VMEM_LIMIT_BYTES = 56 * 1024 * 1024

W_CAST_ROWS = 256
ROW_TILE = 1024
COL_TILE = 1024

QKVZ_WIDTH = 4 * GDN_WIDTH
SRC_BA = QKVZ_WIDTH
SRC_REST = SRC_BA + 2 * GDN_HEADS
REST_SHIFT = SRC_REST - SRC_BA
REST_WIDTH = SWA_Q_WIDTH + 2 * SWA_KV_WIDTH + 2 * D_MODEL
D_IN = SRC_REST + REST_WIDTH
ROFF_QS = 0
ROFF_KS = ROFF_QS + SWA_Q_WIDTH
ROFF_VS = ROFF_KS + SWA_KV_WIDTH
ROFF_GATES = ROFF_VS + SWA_KV_WIDTH


def _params(semantics):
    return pltpu.CompilerParams(dimension_semantics=semantics, vmem_limit_bytes=VMEM_LIMIT_BYTES)


def _rmsnorm_kernel(x_ref, g_ref, o_ref):
    x = x_ref[...]
    ms = jnp.mean(x * x, axis=-1, keepdims=True)
    o_ref[...] = (x * lax.rsqrt(ms + NORM_EPS) * g_ref[...]).astype(o_ref.dtype)


def _rmsnorm(x, gain, out_dtype, rows=512):
    n, d = x.shape
    return pl.pallas_call(
        _rmsnorm_kernel,
        grid=(n // rows,),
        in_specs=[pl.BlockSpec((rows, d), lambda i: (i, 0)),
                  pl.BlockSpec((1, d), lambda i: (0, 0))],
        out_specs=pl.BlockSpec((rows, d), lambda i: (i, 0)),
        out_shape=jax.ShapeDtypeStruct((n, d), out_dtype),
        compiler_params=_params(("parallel",)),
        name="rmsnorm",
    )(x, gain.reshape(1, d))


def _mm_kernel(*refs, nk, epilogue):
    if epilogue == "residual":
        a_ref, w_ref, r_ref, o_ref = refs[:4]
        rest = refs[4:]
    else:
        a_ref, w_ref, o_ref = refs[:3]
        r_ref = None
        rest = refs[3:]

    def finish(acc):
        if epilogue == "relu2":
            acc = jnp.square(jnp.maximum(acc, 0.0))
        elif epilogue == "residual":
            acc = r_ref[...] + acc
        o_ref[...] = acc.astype(o_ref.dtype)

    part = jnp.dot(a_ref[...], w_ref[...], preferred_element_type=F32)
    if nk == 1:
        finish(part)
        return
    acc_ref = rest[0]
    k = pl.program_id(2)

    @pl.when(k == 0)
    def _():
        acc_ref[...] = part

    @pl.when(jnp.logical_and(k > 0, k < nk - 1))
    def _():
        acc_ref[...] += part

    @pl.when(k == nk - 1)
    def _():
        finish(acc_ref[...] + part)


def _matmul(a, w, layer, *, out_dtype, tm, tn, tk=None, epilogue="store", res=None, name="matmul"):
    m, kdim = a.shape
    n = w.shape[2]
    tk = kdim if tk is None else tk
    nk = kdim // tk
    grid = (m // tm, pl.cdiv(n, tn), nk)
    in_specs = [pl.BlockSpec((tm, tk), lambda i, j, k: (i, k)),
                pl.BlockSpec((None, tk, tn), lambda i, j, k: (layer, k, j))]
    args = [a, w]
    if epilogue == "residual":
        in_specs.append(pl.BlockSpec((tm, tn), lambda i, j, k: (i, j)))
        args.append(res)
    scratch = [pltpu.VMEM((tm, tn), F32)] if nk > 1 else []
    return pl.pallas_call(
        functools.partial(_mm_kernel, nk=nk, epilogue=epilogue),
        grid=grid,
        in_specs=in_specs,
        out_specs=pl.BlockSpec((tm, tn), lambda i, j, k: (i, j)),
        out_shape=jax.ShapeDtypeStruct((m, n), out_dtype),
        scratch_shapes=scratch,
        compiler_params=_params(("parallel", "parallel", "arbitrary")),
        name=name,
    )(*args)


def _mm_ws_kernel(*refs, epilogue, shift, transposed):
    refs = list(refs)
    a_ref, w_ref = refs[:2]
    pos = 2
    trail_ref = res_ref = None
    if shift:
        trail_ref = refs[pos]
        pos += 1
    if epilogue == "residual":
        res_ref = refs[pos]
        pos += 1
    o_ref, wbf_ref = refs[pos], refs[pos + 1]
    i = pl.program_id(1)
    n_rows = wbf_ref.shape[0]

    @pl.when(i == 0)
    def _():
        body = n_rows - shift
        for r in range(0, body, W_CAST_ROWS):
            rows = min(W_CAST_ROWS, body - r)
            wbf_ref[r:r + rows, :] = w_ref[r + shift:r + shift + rows, :].astype(BF16)
        if shift:
            wbf_ref[body:, :] = trail_ref[...].astype(BF16)

    if transposed:
        acc = lax.dot_general(a_ref[...], wbf_ref[...], (((1,), (1,)), ((), ())),
                              preferred_element_type=F32)
    else:
        acc = jnp.dot(a_ref[...], wbf_ref[...], preferred_element_type=F32)
    if epilogue == "relu2":
        acc = jnp.square(jnp.maximum(acc, 0.0))
    elif epilogue == "residual":
        acc = res_ref[...] + acc
    o_ref[...] = acc.astype(o_ref.dtype)


def _matmul_ws(a, w, layer, *, col0, n_out, out_dtype, tm, tn, epilogue="store", shift=0,
               transposed=False, res=None, name="matmul_ws"):
    m, kdim = a.shape
    blk0 = col0 // tn
    in_specs = [pl.BlockSpec((tm, kdim), lambda j, i: (i, 0))]
    if transposed:
        in_specs.append(pl.BlockSpec((None, tn, kdim), lambda j, i: (layer, blk0 + j, 0)))
    else:
        assert shift == 0
        in_specs.append(pl.BlockSpec((None, kdim, tn), lambda j, i: (layer, 0, blk0 + j)))
    args = [a, w]
    if shift:
        last_blk = (w.shape[1] - 1) // shift
        in_specs.append(pl.BlockSpec(
            (None, shift, kdim),
            lambda j, i: (layer, jnp.minimum((col0 + (j + 1) * tn) // shift, last_blk), 0)))
        args.append(w)
    if epilogue == "residual":
        in_specs.append(pl.BlockSpec((tm, tn), lambda j, i: (i, j)))
        args.append(res)
    wbf_shape = (tn, kdim) if transposed else (kdim, tn)
    return pl.pallas_call(
        functools.partial(_mm_ws_kernel, epilogue=epilogue, shift=shift, transposed=transposed),
        grid=(pl.cdiv(n_out, tn), m // tm),
        in_specs=in_specs,
        out_specs=pl.BlockSpec((tm, tn), lambda j, i: (i, j)),
        out_shape=jax.ShapeDtypeStruct((m, n_out), out_dtype),
        scratch_shapes=[pltpu.VMEM(wbf_shape, BF16)],
        compiler_params=_params(("parallel", "arbitrary")),
        name=name,
    )(*args)


def _proj_norm_kernel(a_ref, w_ref, res_ref, g_ref, x_ref, h_ref, wbf_ref):
    @pl.when(pl.program_id(0) == 0)
    def _():
        for r in range(0, wbf_ref.shape[0], W_CAST_ROWS):
            rows = slice(r, r + W_CAST_ROWS)
            wbf_ref[rows, :] = w_ref[rows, :].astype(BF16)

    x = res_ref[...] + jnp.dot(a_ref[...], wbf_ref[...], preferred_element_type=F32)
    x_ref[...] = x
    ms = jnp.mean(x * x, axis=-1, keepdims=True)
    h_ref[...] = (x * lax.rsqrt(ms + NORM_EPS) * g_ref[...]).astype(h_ref.dtype)


def _proj_norm(a, w, layer, res, gain, tm):
    m, kdim = a.shape
    d = w.shape[2]
    return pl.pallas_call(
        _proj_norm_kernel,
        grid=(m // tm,),
        in_specs=[pl.BlockSpec((tm, kdim), lambda i: (i, 0)),
                  pl.BlockSpec((None, kdim, d), lambda i: (layer, 0, 0), pipeline_mode=pl.Buffered(1)),
                  pl.BlockSpec((tm, d), lambda i: (i, 0)),
                  pl.BlockSpec((1, d), lambda i: (0, 0))],
        out_specs=[pl.BlockSpec((tm, d), lambda i: (i, 0)),
                   pl.BlockSpec((tm, d), lambda i: (i, 0))],
        out_shape=[jax.ShapeDtypeStruct((m, d), F32), jax.ShapeDtypeStruct((m, d), BF16)],
        scratch_shapes=[pltpu.VMEM((kdim, d), BF16)],
        compiler_params=_params(("arbitrary",)),
        name="out_proj_norm",
    )(a, w, res, gain.reshape(1, d))


def _sigmoid(x):
    return 0.5 * jnp.tanh(0.5 * x) + 0.5


def _silu(x):
    u = 0.5 * x
    return u * jnp.tanh(u) + u


def _softplus(x):
    return jnp.maximum(x, 0.0) + jnp.log(1.0 + jnp.exp(-jnp.abs(x)))


def _cumsum_rows(x):
    rows = x.shape[0]
    row = lax.broadcasted_iota(jnp.int32, x.shape, 0)
    s = 1
    while s < rows:
        x = x + jnp.where(row >= s, pltpu.roll(x, s, axis=0), 0.0)
        s *= 2
    return x


def _shift_rows(x, halo, s):
    xr = pltpu.roll(x, s, axis=0)
    hr = pltpu.roll(halo, s, axis=0)
    row = lax.broadcasted_iota(jnp.int32, halo.shape, 0)
    first = jnp.where(row < s, hr, xr[:SUBLANES])
    return jnp.concatenate([first, xr[SUBLANES:]], axis=0)


def _conv_silu(x, halo, w):
    acc = x * w[CONV_K - 1:CONV_K]
    for s in range(1, CONV_K):
        acc = acc + _shift_rows(x, halo, s) * w[CONV_K - 1 - s:CONV_K - s]
    return _silu(acc)


def _dot(a, b):
    return jnp.dot(a.astype(BF16), b.astype(BF16), preferred_element_type=F32)


def _dot_nt(a, b):
    return lax.dot_general(a.astype(BF16), b.astype(BF16), (((1,), (1,)), ((), ())),
                           preferred_element_type=F32)


def _dot_tn(a, b):
    return lax.dot_general(a.astype(BF16), b.astype(BF16), (((0,), (0,)), ((), ())),
                           preferred_element_type=F32)


def _split_bf16(x):
    hi = x.astype(BF16)
    return hi, (x - hi.astype(F32)).astype(BF16)


def _block_diag_pair(xp, lane_lo):
    zero = jnp.zeros_like(xp)
    return jnp.concatenate([jnp.where(lane_lo, xp, zero), jnp.where(lane_lo, zero, xp)], axis=0)


def _pair_dot3(l_parts, x_parts, lane_lo):
    l_hi, l_lo = l_parts
    x_hi, x_lo = x_parts
    bd_hi = _block_diag_pair(x_hi, lane_lo)
    bd_lo = _block_diag_pair(x_lo, lane_lo)
    lhs = jnp.concatenate([l_hi, l_lo, l_hi], axis=1)
    rhs = jnp.concatenate([bd_hi, bd_hi, bd_lo], axis=0)
    return jnp.dot(lhs, rhs, preferred_element_type=F32)


def _gdn_kernel(qkv_ref, z_ref, ba_ref, convw_ref, gp_ref, gnorm_ref, o_ref, state_ref, halo_ref):
    @pl.when(pl.program_id(1) == 0)
    def _():
        state_ref[...] = jnp.zeros_like(state_ref)
        halo_ref[...] = jnp.zeros_like(halo_ref)

    def chunk(ci, carry):
        rows = pl.ds(pl.multiple_of(ci * CHUNK, CHUNK), CHUNK)
        _gdn_chunk(rows, qkv_ref, z_ref, ba_ref, convw_ref, gp_ref, gnorm_ref, o_ref, state_ref,
                   halo_ref)
        tail = pl.ds(pl.multiple_of(ci * CHUNK + CHUNK - SUBLANES, SUBLANES), SUBLANES)
        halo_ref[...] = qkv_ref[0, tail, :]
        return carry

    lax.fori_loop(0, GDN_CHUNKS_PER_STEP, chunk, 0)


def _gdn_chunk(rows, qkv_ref, z_ref, ba_ref, convw_ref, gp_ref, gnorm_ref, o_ref, state_ref, halo_ref):
    ba = ba_ref[0, rows, :]
    beta_all = _sigmoid(ba)
    g_all = -jnp.exp(gp_ref[0:1]) * _softplus(ba + gp_ref[1:2])
    decay_all = _cumsum_rows(g_all)
    decay_t2 = jnp.transpose(jnp.concatenate([decay_all, decay_all], axis=0))

    ri = lax.broadcasted_iota(jnp.int32, (CHUNK, 2 * CHUNK), 0)
    ci = lax.broadcasted_iota(jnp.int32, (CHUNK, 2 * CHUNK), 1)
    lane_lo = ci < CHUNK
    cj = jnp.where(lane_lo, ci, ci - CHUNK)
    causal = ri >= cj
    strict = ri > cj
    lane_lo_row = lane_lo[0:1]
    gnorm = gnorm_ref[...]
    q_scale = GDN_HEAD_DIM ** -0.5
    zeros_hd = jnp.zeros((CHUNK, GDN_HEAD_DIM), F32)
    zeros_rhs = jnp.zeros((CHUNK, 2 * GDN_HEAD_DIM), F32)

    def head_inputs(h):
        lo = h * GDN_HEAD_DIM
        sl_q = slice(lo, lo + GDN_HEAD_DIM)
        sl_k = slice(GDN_WIDTH + lo, GDN_WIDTH + lo + GDN_HEAD_DIM)
        sl_v = slice(2 * GDN_WIDTH + lo, 2 * GDN_WIDTH + lo + GDN_HEAD_DIM)
        qc = _conv_silu(qkv_ref[0, rows, sl_q], halo_ref[:, sl_q], convw_ref[:, sl_q])
        kc = _conv_silu(qkv_ref[0, rows, sl_k], halo_ref[:, sl_k], convw_ref[:, sl_k])
        vc = _conv_silu(qkv_ref[0, rows, sl_v], halo_ref[:, sl_v], convw_ref[:, sl_v])
        qn = qc * (lax.rsqrt(jnp.sum(qc * qc, axis=-1, keepdims=True) + NORM_EPS) * q_scale)
        kn = kc * lax.rsqrt(jnp.sum(kc * kc, axis=-1, keepdims=True) + NORM_EPS)
        beta = beta_all[:, h:h + 1]
        dcol = decay_all[:, GDN_HEADS + h:GDN_HEADS + h + 1]
        dlast = dcol[CHUNK - 1:CHUNK]
        edec = jnp.exp(dcol)
        kb = kn * beta
        return dict(sl=sl_q, qn=qn, kn=kn, kb=kb, dcol=dcol, q_dec=qn * edec,
                    k_dec=kn * jnp.exp(dlast - dcol), sdec=jnp.exp(dlast),
                    rhs=jnp.concatenate([vc * beta, kb * edec], axis=1))

    for g0 in range(0, GDN_HEADS, GDN_HEAD_GROUP):
        pairs = [(h, h + 1) for h in range(g0, g0 + GDN_HEAD_GROUP, 2)]
        pre = [(head_inputs(h0), head_inputs(h1)) for h0, h1 in pairs]

        gamma, r = [], []
        for (h0, h1), (p0, p1) in zip(pairs, pre):
            drow = jnp.where(lane_lo_row, decay_t2[GDN_HEADS + h0:GDN_HEADS + h0 + 1],
                             decay_t2[GDN_HEADS + h1:GDN_HEADS + h1 + 1])
            dcol = jnp.where(lane_lo, p0["dcol"], p1["dcol"])
            gamma.append(jnp.exp(jnp.where(causal, dcol - drow, -jnp.inf)))
            lhs = jnp.concatenate([jnp.concatenate([p0["kb"], p1["kb"]], axis=1),
                                   jnp.concatenate([p0["qn"], p1["qn"]], axis=1)], axis=0)
            k_bd = jnp.concatenate([jnp.concatenate([p0["kn"], zeros_hd], axis=1),
                                    jnp.concatenate([zeros_hd, p1["kn"]], axis=1)], axis=0)
            r.append(_dot_nt(lhs, k_bd))
        a_low = [jnp.where(strict, ri_[:CHUNK] * gm, 0.0) for ri_, gm in zip(r, gamma)]
        qk = [ri_[CHUNK:] * gm for ri_, gm in zip(r, gamma)]

        nmat = [-a for a in a_low]
        a_parts = [_split_bf16(a) for a in a_low]
        xpow = [_pair_dot3(ap, ap, lane_lo) for ap in a_parts]
        p2 = 2
        while 2 * p2 < CHUNK:
            n_parts = [_split_bf16(nm) for nm in nmat]
            x_parts = [_split_bf16(xp) for xp in xpow]
            prod = [_pair_dot3(tuple(jnp.concatenate([n_, x_], axis=0) for n_, x_ in zip(npt, xpt)),
                               xpt, lane_lo) for npt, xpt in zip(n_parts, x_parts)]
            nmat = [nm + xp + pr[:CHUNK] for nm, xp, pr in zip(nmat, xpow, prod)]
            xpow = [pr[CHUNK:] for pr in prod]
            p2 *= 2
        prod = [_pair_dot3(_split_bf16(nm), _split_bf16(xp), lane_lo) for nm, xp in zip(nmat, xpow)]
        nmat = [nm + xp + pr for nm, xp, pr in zip(nmat, xpow, prod)]

        heads, hp, uw = [], [], []
        for (h0, h1), (p0, p1), nm in zip(pairs, pre, nmat):
            n0 = jnp.where(lane_lo, nm, 0.0)
            n1 = jnp.where(lane_lo, 0.0, nm)
            uw.append(p0["rhs"] + _dot(n0, jnp.concatenate([p0["rhs"], zeros_rhs], axis=0)))
            uw.append(p1["rhs"] + _dot(n1, jnp.concatenate([zeros_rhs, p1["rhs"]], axis=0)))
            heads += [h0, h1]
            hp += [p0, p1]
        states = [state_ref[h] for h in heads]
        ws_qs = [_dot(jnp.concatenate([uwi[:, GDN_HEAD_DIM:], p["q_dec"]], axis=0), st)
                 for uwi, p, st in zip(uw, hp, states)]
        v_new = [uwi[:, :GDN_HEAD_DIM] - wq[:CHUNK] for uwi, wq in zip(uw, ws_qs)]
        o_att = []
        for i, qki in enumerate(qk):
            v_bd = jnp.concatenate([jnp.concatenate([v_new[2 * i], zeros_hd], axis=1),
                                    jnp.concatenate([zeros_hd, v_new[2 * i + 1]], axis=1)], axis=0)
            oa = _dot(qki, v_bd)
            o_att += [oa[:, :GDN_HEAD_DIM], oa[:, GDN_HEAD_DIM:]]
        kv_upd = [_dot_tn(p["k_dec"], vn) for p, vn in zip(hp, v_new)]
        for h, p, st, wq, oa, kvu in zip(heads, hp, states, ws_qs, o_att, kv_upd):
            state_ref[h] = st * p["sdec"] + kvu
            o = wq[CHUNK:] + oa
            on = o * lax.rsqrt(jnp.mean(o * o, axis=-1, keepdims=True) + NORM_EPS) * gnorm
            zc = z_ref[0, rows, p["sl"]]
            o_ref[0, rows, p["sl"]] = (on * _silu(zc)).astype(o_ref.dtype)


def _gdn(qkvz, ba, conv_w, a_log, dt_bias, gnorm):
    b, t, _ = qkvz.shape
    gp = jnp.zeros((2, LANES), F32)
    gp = gp.at[0, GDN_HEADS:2 * GDN_HEADS].set(a_log).at[1, GDN_HEADS:2 * GDN_HEADS].set(dt_bias)
    qkv_w = 3 * GDN_WIDTH
    step_rows = GDN_CHUNKS_PER_STEP * CHUNK
    return pl.pallas_call(
        _gdn_kernel,
        grid=(b, t // step_rows),
        in_specs=[
            pl.BlockSpec((1, step_rows, qkv_w), lambda i, c: (i, c, 0)),
            pl.BlockSpec((1, step_rows, GDN_WIDTH), lambda i, c: (i, c, qkv_w // GDN_WIDTH)),
            pl.BlockSpec((1, step_rows, LANES), lambda i, c: (i, c, 0)),
            pl.BlockSpec((CONV_K, qkv_w), lambda i, c: (0, 0)),
            pl.BlockSpec((2, LANES), lambda i, c: (0, 0)),
            pl.BlockSpec((1, GDN_HEAD_DIM), lambda i, c: (0, 0)),
        ],
        out_specs=pl.BlockSpec((1, step_rows, GDN_WIDTH), lambda i, c: (i, c, 0)),
        out_shape=jax.ShapeDtypeStruct((b, t, GDN_WIDTH), BF16),
        scratch_shapes=[pltpu.VMEM((GDN_HEADS, GDN_HEAD_DIM, GDN_HEAD_DIM), F32),
                        pltpu.VMEM((SUBLANES, qkv_w), F32)],
        compiler_params=_params(("parallel", "arbitrary")),
        name="gdn",
    )(qkvz, qkvz, ba, conv_w, gp, gnorm.reshape(1, GDN_HEAD_DIM))


def _swa_kernel(sink_ref, q_ref, kp_ref, kc_ref, vp_ref, vc_ref, o_ref):
    nb = pl.program_id(1)
    kj = lax.broadcasted_iota(jnp.int32, (WINDOW, WINDOW), 0)
    qi = lax.broadcasted_iota(jnp.int32, (WINDOW, WINDOW), 1)
    from_cur = kj <= qi
    dist = jnp.where(from_cur, qi - kj, qi - kj + WINDOW).astype(F32)
    dist = jnp.where(jnp.logical_and(nb == 0, jnp.logical_not(from_cur)), jnp.inf, dist)
    lane_lo = qi < SWA_HEAD_DIM
    q_scale = SWA_HEAD_DIM ** -0.5
    pair_w = 2 * SWA_HEAD_DIM

    def lo_hi(tile, kv):
        swapped = pltpu.roll(tile, SWA_HEAD_DIM, axis=1)
        zero = jnp.zeros_like(tile)
        if kv % 2 == 0:
            return jnp.where(lane_lo, tile, zero), jnp.where(lane_lo, zero, swapped)
        return jnp.where(lane_lo, swapped, zero), jnp.where(lane_lo, zero, tile)

    batches = []
    for kv in range(SWA_KV_HEADS):
        tile_sl = slice((kv // 2) * pair_w, (kv // 2 + 1) * pair_w)
        kc_lo, kc_hi = lo_hi(kc_ref[0, :, tile_sl], kv)
        kp_lo, kp_hi = lo_hi(kp_ref[0, :, tile_sl], kv)
        kc2 = jnp.concatenate([kc_lo, kc_hi], axis=0).astype(BF16)
        kp2 = jnp.concatenate([kp_lo, kp_hi], axis=0).astype(BF16)
        v2 = jnp.concatenate([vc_ref[0, :, tile_sl], vp_ref[0, :, tile_sl]], axis=0).astype(BF16)
        row0 = (kv % 2) * SWA_HEAD_DIM
        for pp in range(SWA_GROUP // 4):
            head0 = kv * SWA_GROUP + 4 * pp
            col = head0 * SWA_HEAD_DIM
            q2 = jnp.concatenate([q_ref[0, :, col:col + pair_w],
                                  q_ref[0, :, col + pair_w:col + 2 * pair_w]], axis=0)
            q2 = (q2 * q_scale).astype(BF16)
            s_cur = lax.dot_general(kc2, q2, (((1,), (1,)), ((), ())), preferred_element_type=F32)
            s_prev = lax.dot_general(kp2, q2, (((1,), (1,)), ((), ())), preferred_element_type=F32)
            batches.append((head0, col, v2, row0, s_cur, s_prev))

    pairs = []
    for head0, col, v2, row0, s_cur, s_prev in batches:
        for pr in range(2):
            cs = slice(pr * WINDOW, (pr + 1) * WINDOW)
            p_cur, p_prev, inv_l = [], [], []
            for half in range(2):
                head = head0 + 2 * pr + half
                rs = slice(half * WINDOW, (half + 1) * WINDOW)
                slope = 2.0 ** (-8.0 * (head + 1) / SWA_Q_HEADS)
                s = jnp.where(from_cur, s_cur[rs, cs], s_prev[rs, cs]) - slope * dist
                sink = sink_ref[0, head]
                m = jnp.maximum(jnp.max(s, axis=0, keepdims=True), sink)
                p = jnp.exp(s - m)
                inv_l.append(1.0 / (jnp.sum(p, axis=0, keepdims=True) + jnp.exp(sink - m)))
                pb = p.astype(BF16)
                zero = jnp.zeros_like(pb)
                p_cur.append(jnp.where(from_cur, pb, zero))
                p_prev.append(jnp.where(from_cur, zero, pb))
            probs = jnp.concatenate([jnp.concatenate(p_cur, axis=1),
                                     jnp.concatenate(p_prev, axis=1)], axis=0)
            pairs.append((col + pr * pair_w, v2, row0, probs, inv_l))

    outs = [lax.dot_general(v2, probs, (((0,), (0,)), ((), ())), preferred_element_type=F32)
            for _, v2, _, probs, _ in pairs]
    for (pcol, _, row0, _, inv_l), o_t in zip(pairs, outs):
        o_t = o_t[row0:row0 + SWA_HEAD_DIM]
        o_pair = jnp.concatenate([o_t[:, :WINDOW] * inv_l[0],
                                  o_t[:, WINDOW:] * inv_l[1]], axis=0)
        o_ref[0, :, pcol:pcol + pair_w] = jnp.transpose(o_pair).astype(o_ref.dtype)


def _swa(rest, sinks):
    b, t, _ = rest.shape
    kb = ROFF_KS // SWA_KV_WIDTH
    vb = ROFF_VS // SWA_KV_WIDTH
    prev = lambda n: jnp.maximum(n - 1, 0)
    return pl.pallas_call(
        _swa_kernel,
        grid=(b, t // WINDOW),
        in_specs=[
            pl.BlockSpec(memory_space=pltpu.SMEM),
            pl.BlockSpec((1, WINDOW, SWA_Q_WIDTH), lambda i, n: (i, n, ROFF_QS // SWA_Q_WIDTH)),
            pl.BlockSpec((1, WINDOW, SWA_KV_WIDTH), lambda i, n: (i, prev(n), kb)),
            pl.BlockSpec((1, WINDOW, SWA_KV_WIDTH), lambda i, n: (i, n, kb)),
            pl.BlockSpec((1, WINDOW, SWA_KV_WIDTH), lambda i, n: (i, prev(n), vb)),
            pl.BlockSpec((1, WINDOW, SWA_KV_WIDTH), lambda i, n: (i, n, vb)),
        ],
        out_specs=pl.BlockSpec((1, WINDOW, SWA_Q_WIDTH), lambda i, n: (i, n, 0)),
        out_shape=jax.ShapeDtypeStruct((b, t, SWA_Q_WIDTH), BF16),
        compiler_params=_params(("parallel", "parallel")),
        name="swa",
    )(sinks.reshape(1, SWA_Q_HEADS), rest, rest, rest, rest, rest)


def _merge_kernel(og_ref, os_ref, wg_ref, ws_ref, gg_ref, gs_ref, o_ref, wg_bf, ws_bf):
    @pl.when(pl.program_id(1) == 0)
    def _():
        for r in range(0, wg_bf.shape[0], W_CAST_ROWS):
            rows = slice(r, r + W_CAST_ROWS)
            wg_bf[rows, :] = wg_ref[rows, :].astype(BF16)
            ws_bf[rows, :] = ws_ref[rows, :].astype(BF16)

    yg = jnp.dot(og_ref[...], wg_bf[...], preferred_element_type=F32)
    ys = jnp.dot(os_ref[...], ws_bf[...], preferred_element_type=F32)
    o_ref[...] = (_sigmoid(gg_ref[...]) * yg + _sigmoid(gs_ref[...]) * ys).astype(o_ref.dtype)


def _merge(o_gdn, o_swa, w_gdn, w_swa, layer, rest2d, tm, tn=512):
    m = o_gdn.shape[0]
    gate_blk = ROFF_GATES // tn
    return pl.pallas_call(
        _merge_kernel,
        grid=(D_MODEL // tn, m // tm),
        in_specs=[
            pl.BlockSpec((tm, GDN_WIDTH), lambda j, i: (i, 0)),
            pl.BlockSpec((tm, SWA_Q_WIDTH), lambda j, i: (i, 0)),
            pl.BlockSpec((None, GDN_WIDTH, tn), lambda j, i: (layer, 0, j)),
            pl.BlockSpec((None, SWA_Q_WIDTH, tn), lambda j, i: (layer, 0, j)),
            pl.BlockSpec((tm, tn), lambda j, i: (i, gate_blk + j)),
            pl.BlockSpec((tm, tn), lambda j, i: (i, gate_blk + D_MODEL // tn + j)),
        ],
        out_specs=pl.BlockSpec((tm, tn), lambda j, i: (i, j)),
        out_shape=jax.ShapeDtypeStruct((m, D_MODEL), BF16),
        scratch_shapes=[pltpu.VMEM((GDN_WIDTH, tn), BF16), pltpu.VMEM((SWA_Q_WIDTH, tn), BF16)],
        compiler_params=_params(("parallel", "arbitrary")),
        name="merge",
    )(o_gdn, o_swa, w_gdn, w_swa, rest2d, rest2d)


def kernel(x, norm1_g, w_in, conv_w, a_log, dt_bias, gdn_norm_g, attn_sinks, w_branch_gdn,
           w_branch_swa, w_out, norm2_g, w_ff_up, w_ff_down, final_norm_g):
    b, t, d = x.shape
    depth = w_in.shape[0]
    n = b * t
    tm, tn = ROW_TILE, COL_TILE
    w_in_t = jnp.swapaxes(w_in, 1, 2)
    w_down = w_ff_down.astype(BF16)

    xf = x.reshape(n, d)
    for l in range(depth):
        h = _rmsnorm(xf, norm1_g[l], BF16)
        qkvz = _matmul_ws(h, w_in_t, l, col0=0, n_out=QKVZ_WIDTH, out_dtype=F32, tm=tm, tn=tn,
                          transposed=True, name="in_proj_qkvz")
        ba = _matmul_ws(h, w_in_t, l, col0=SRC_BA, n_out=LANES, out_dtype=F32, tm=tm, tn=LANES,
                        transposed=True, name="in_proj_ba")
        rest = _matmul_ws(h, w_in_t, l, col0=SRC_BA, n_out=REST_WIDTH, out_dtype=F32, tm=tm, tn=tn,
                          shift=REST_SHIFT, transposed=True, name="in_proj_rest")
        o_gdn = _gdn(qkvz.reshape(b, t, QKVZ_WIDTH), ba.reshape(b, t, LANES), conv_w[l], a_log[l],
                     dt_bias[l], gdn_norm_g[l]).reshape(n, GDN_WIDTH)
        o_swa = _swa(rest.reshape(b, t, REST_WIDTH), attn_sinks[l]).reshape(n, SWA_Q_WIDTH)
        merged = _merge(o_gdn, o_swa, w_branch_gdn, w_branch_swa, l, rest, tm=tm)
        xf, h2 = _proj_norm(merged, w_out, l, xf, norm2_g[l], tm=tm // 2)
        act = _matmul_ws(h2, w_ff_up, l, col0=0, n_out=D_FF, out_dtype=BF16, tm=tm, tn=tn,
                         epilogue="relu2", name="ff_up")
        xf = _matmul(act, w_down, l, out_dtype=F32, tm=tm, tn=tn, tk=2048, epilogue="residual",
                     res=xf, name="ff_down")
    return _rmsnorm(xf, final_norm_g, F32).reshape(b, t, d)
```

```python
import functools

import jax
import jax.numpy as jnp
from jax import lax
from jax.experimental import pallas as pl
from jax.experimental.pallas import tpu as pltpu

F32 = jnp.float32
BF16 = jnp.bfloat16

D_MODEL = 2048
GDN_HEAD_DIM = 128
GDN_HEADS = 16
GDN_WIDTH = GDN_HEADS * GDN_HEAD_DIM
CONV_K = 4
CHUNK = 64
GDN_HEAD_GROUP = 16
GDN_CHUNKS_PER_STEP = 2
SWA_HEAD_DIM = 64
SWA_Q_HEADS = 32
SWA_KV_HEADS = 4
SWA_GROUP = SWA_Q_HEADS // SWA_KV_HEADS
SWA_Q_WIDTH = SWA_Q_HEADS * SWA_HEAD_DIM
SWA_KV_WIDTH = SWA_KV_HEADS * SWA_HEAD_DIM
WINDOW = 128
D_FF = 4 * D_MODEL
NORM_EPS = 1e-6

LANES = 128
SUBLANES = 8
VMEM_LIMIT_BYTES = 56 * 1024 * 1024

W_CAST_ROWS = 256
ROW_TILE = 1024
COL_TILE = 1024

QKVZ_WIDTH = 4 * GDN_WIDTH
SRC_BA = QKVZ_WIDTH
SRC_REST = SRC_BA + 2 * GDN_HEADS
REST_SHIFT = SRC_REST - SRC_BA
REST_WIDTH = SWA_Q_WIDTH + 2 * SWA_KV_WIDTH + 2 * D_MODEL
D_IN = SRC_REST + REST_WIDTH
ROFF_QS = 0
ROFF_KS = ROFF_QS + SWA_Q_WIDTH
ROFF_VS = ROFF_KS + SWA_KV_WIDTH
ROFF_GATES = ROFF_VS + SWA_KV_WIDTH


def _params(semantics):
    return pltpu.CompilerParams(dimension_semantics=semantics, vmem_limit_bytes=VMEM_LIMIT_BYTES)


def _rmsnorm_kernel(x_ref, g_ref, o_ref):
    x = x_ref[...]
    ms = jnp.mean(x * x, axis=-1, keepdims=True)
    o_ref[...] = (x * lax.rsqrt(ms + NORM_EPS) * g_ref[...]).astype(o_ref.dtype)


def _rmsnorm(x, gain, out_dtype, rows=512):
    n, d = x.shape
    return pl.pallas_call(
        _rmsnorm_kernel,
        grid=(n // rows,),
        in_specs=[pl.BlockSpec((rows, d), lambda i: (i, 0)),
                  pl.BlockSpec((1, d), lambda i: (0, 0))],
        out_specs=pl.BlockSpec((rows, d), lambda i: (i, 0)),
        out_shape=jax.ShapeDtypeStruct((n, d), out_dtype),
        compiler_params=_params(("parallel",)),
        name="rmsnorm",
    )(x, gain.reshape(1, d))


def _mm_kernel(*refs, nk, epilogue):
    if epilogue == "residual":
        a_ref, w_ref, r_ref, o_ref = refs[:4]
        rest = refs[4:]
    else:
        a_ref, w_ref, o_ref = refs[:3]
        r_ref = None
        rest = refs[3:]

    def finish(acc):
        if epilogue == "relu2":
            acc = jnp.square(jnp.maximum(acc, 0.0))
        elif epilogue == "residual":
            acc = r_ref[...] + acc
        o_ref[...] = acc.astype(o_ref.dtype)

    part = jnp.dot(a_ref[...], w_ref[...], preferred_element_type=F32)
    if nk == 1:
        finish(part)
        return
    acc_ref = rest[0]
    k = pl.program_id(2)

    @pl.when(k == 0)
    def _():
        acc_ref[...] = part

    @pl.when(jnp.logical_and(k > 0, k < nk - 1))
    def _():
        acc_ref[...] += part

    @pl.when(k == nk - 1)
    def _():
        finish(acc_ref[...] + part)


def _matmul(a, w, layer, *, out_dtype, tm, tn, tk=None, epilogue="store", res=None, name="matmul"):
    m, kdim = a.shape
    n = w.shape[2]
    tk = kdim if tk is None else tk
    nk = kdim // tk
    grid = (m // tm, pl.cdiv(n, tn), nk)
    in_specs = [pl.BlockSpec((tm, tk), lambda i, j, k: (i, k)),
                pl.BlockSpec((None, tk, tn), lambda i, j, k: (layer, k, j))]
    args = [a, w]
    if epilogue == "residual":
        in_specs.append(pl.BlockSpec((tm, tn), lambda i, j, k: (i, j)))
        args.append(res)
    scratch = [pltpu.VMEM((tm, tn), F32)] if nk > 1 else []
    return pl.pallas_call(
        functools.partial(_mm_kernel, nk=nk, epilogue=epilogue),
        grid=grid,
        in_specs=in_specs,
        out_specs=pl.BlockSpec((tm, tn), lambda i, j, k: (i, j)),
        out_shape=jax.ShapeDtypeStruct((m, n), out_dtype),
        scratch_shapes=scratch,
        compiler_params=_params(("parallel", "parallel", "arbitrary")),
        name=name,
    )(*args)


def _mm_ws_kernel(*refs, epilogue, shift, transposed, extra_at):
    refs = list(refs)
    a_ref, w_ref = refs[:2]
    pos = 2
    trail_ref = res_ref = extra_ref = None
    if shift:
        trail_ref = refs[pos]
        pos += 1
    if extra_at is not None:
        extra_ref = refs[pos]
        pos += 1
    if epilogue == "residual":
        res_ref = refs[pos]
        pos += 1
    o_ref, wbf_ref = refs[pos], refs[pos + 1]
    i = pl.program_id(1)
    n_rows = wbf_ref.shape[0]

    @pl.when(i == 0)
    def _():
        body = n_rows - shift
        for r in range(0, body, W_CAST_ROWS):
            rows = min(W_CAST_ROWS, body - r)
            wbf_ref[r:r + rows, :] = w_ref[r + shift:r + shift + rows, :].astype(BF16)
        if shift:
            wbf_ref[body:, :] = trail_ref[...].astype(BF16)
        if extra_at is not None:
            @pl.when(pl.program_id(0) == pl.num_programs(0) - 1)
            def _():
                wbf_ref[extra_at:extra_at + LANES, :] = extra_ref[...].astype(BF16)

    if transposed:
        acc = lax.dot_general(a_ref[...], wbf_ref[...], (((1,), (1,)), ((), ())),
                              preferred_element_type=F32)
    else:
        acc = jnp.dot(a_ref[...], wbf_ref[...], preferred_element_type=F32)
    if epilogue == "relu2":
        acc = jnp.square(jnp.maximum(acc, 0.0))
    elif epilogue == "residual":
        acc = res_ref[...] + acc
    o_ref[...] = acc.astype(o_ref.dtype)


def _matmul_ws(a, w, layer, *, col0, n_out, out_dtype, tm, tn, epilogue="store", shift=0,
               transposed=False, res=None, extra_col0=None, name="matmul_ws"):
    m, kdim = a.shape
    blk0 = col0 // tn
    in_specs = [pl.BlockSpec((tm, kdim), lambda j, i: (i, 0))]
    if transposed:
        in_specs.append(pl.BlockSpec((None, tn, kdim), lambda j, i: (layer, blk0 + j, 0)))
    else:
        assert shift == 0
        in_specs.append(pl.BlockSpec((None, kdim, tn), lambda j, i: (layer, 0, blk0 + j)))
    args = [a, w]
    if shift:
        last_blk = (w.shape[1] - 1) // shift
        in_specs.append(pl.BlockSpec(
            (None, shift, kdim),
            lambda j, i: (layer, jnp.minimum((col0 + (j + 1) * tn) // shift, last_blk), 0)))
        args.append(w)
    extra_at = None
    if extra_col0 is not None:
        assert transposed and (n_out - LANES) % tn + LANES <= tn
        extra_at = (n_out - LANES) % tn
        in_specs.append(pl.BlockSpec((None, LANES, kdim), lambda j, i: (layer, extra_col0 // LANES, 0)))
        args.append(w)
    if epilogue == "residual":
        in_specs.append(pl.BlockSpec((tm, tn), lambda j, i: (i, j)))
        args.append(res)
    wbf_shape = (tn, kdim) if transposed else (kdim, tn)
    return pl.pallas_call(
        functools.partial(_mm_ws_kernel, epilogue=epilogue, shift=shift, transposed=transposed,
                          extra_at=extra_at),
        grid=(pl.cdiv(n_out, tn), m // tm),
        in_specs=in_specs,
        out_specs=pl.BlockSpec((tm, tn), lambda j, i: (i, j)),
        out_shape=jax.ShapeDtypeStruct((m, n_out), out_dtype),
        scratch_shapes=[pltpu.VMEM(wbf_shape, BF16)],
        compiler_params=_params(("parallel", "arbitrary")),
        name=name,
    )(*args)


def _proj_norm_kernel(a_ref, w_ref, res_ref, g_ref, x_ref, h_ref, wbf_ref):
    @pl.when(pl.program_id(0) == 0)
    def _():
        for r in range(0, wbf_ref.shape[0], W_CAST_ROWS):
            rows = slice(r, r + W_CAST_ROWS)
            wbf_ref[rows, :] = w_ref[rows, :].astype(BF16)

    x = res_ref[...] + jnp.dot(a_ref[...], wbf_ref[...], preferred_element_type=F32)
    x_ref[...] = x
    ms = jnp.mean(x * x, axis=-1, keepdims=True)
    h_ref[...] = (x * lax.rsqrt(ms + NORM_EPS) * g_ref[...]).astype(h_ref.dtype)


def _proj_norm(a, w, layer, res, gain, tm):
    m, kdim = a.shape
    d = w.shape[2]
    return pl.pallas_call(
        _proj_norm_kernel,
        grid=(m // tm,),
        in_specs=[pl.BlockSpec((tm, kdim), lambda i: (i, 0)),
                  pl.BlockSpec((None, kdim, d), lambda i: (layer, 0, 0), pipeline_mode=pl.Buffered(1)),
                  pl.BlockSpec((tm, d), lambda i: (i, 0)),
                  pl.BlockSpec((1, d), lambda i: (0, 0))],
        out_specs=[pl.BlockSpec((tm, d), lambda i: (i, 0)),
                   pl.BlockSpec((tm, d), lambda i: (i, 0))],
        out_shape=[jax.ShapeDtypeStruct((m, d), F32), jax.ShapeDtypeStruct((m, d), BF16)],
        scratch_shapes=[pltpu.VMEM((kdim, d), BF16)],
        compiler_params=_params(("arbitrary",)),
        name="out_proj_norm",
    )(a, w, res, gain.reshape(1, d))


def _sigmoid(x):
    return 0.5 * jnp.tanh(0.5 * x) + 0.5


def _silu(x):
    u = 0.5 * x
    return u * jnp.tanh(u) + u


def _softplus(x):
    return jnp.maximum(x, 0.0) + jnp.log(1.0 + jnp.exp(-jnp.abs(x)))


def _cumsum_rows(x):
    rows = x.shape[0]
    row = lax.broadcasted_iota(jnp.int32, x.shape, 0)
    s = 1
    while s < rows:
        x = x + jnp.where(row >= s, pltpu.roll(x, s, axis=0), 0.0)
        s *= 2
    return x


def _shift_rows(x, halo, s):
    xr = pltpu.roll(x, s, axis=0)
    hr = pltpu.roll(halo, s, axis=0)
    row = lax.broadcasted_iota(jnp.int32, halo.shape, 0)
    first = jnp.where(row < s, hr, xr[:SUBLANES])
    return jnp.concatenate([first, xr[SUBLANES:]], axis=0)


def _conv_silu(x, halo, w):
    acc = x * w[CONV_K - 1:CONV_K]
    for s in range(1, CONV_K):
        acc = acc + _shift_rows(x, halo, s) * w[CONV_K - 1 - s:CONV_K - s]
    return _silu(acc)


def _dot(a, b):
    return jnp.dot(a.astype(BF16), b.astype(BF16), preferred_element_type=F32)


def _dot_nt(a, b):
    return lax.dot_general(a.astype(BF16), b.astype(BF16), (((1,), (1,)), ((), ())),
                           preferred_element_type=F32)


def _dot_tn(a, b):
    return lax.dot_general(a.astype(BF16), b.astype(BF16), (((0,), (0,)), ((), ())),
                           preferred_element_type=F32)


def _split_bf16(x):
    hi = x.astype(BF16)
    return hi, (x - hi.astype(F32)).astype(BF16)


def _block_diag_pair(xp, lane_lo):
    zero = jnp.zeros_like(xp)
    return jnp.concatenate([jnp.where(lane_lo, xp, zero), jnp.where(lane_lo, zero, xp)], axis=0)


def _pair_dot3(l_parts, x_parts, lane_lo):
    l_hi, l_lo = l_parts
    x_hi, x_lo = x_parts
    bd_hi = _block_diag_pair(x_hi, lane_lo)
    bd_lo = _block_diag_pair(x_lo, lane_lo)
    lhs = jnp.concatenate([l_hi, l_lo, l_hi], axis=1)
    rhs = jnp.concatenate([bd_hi, bd_hi, bd_lo], axis=0)
    return jnp.dot(lhs, rhs, preferred_element_type=F32)


def _gdn_kernel(qkv_ref, z_ref, ba_ref, convw_ref, gp_ref, gnorm_ref, o_ref, state_ref, halo_ref):
    @pl.when(pl.program_id(1) == 0)
    def _():
        state_ref[...] = jnp.zeros_like(state_ref)
        halo_ref[...] = jnp.zeros_like(halo_ref)

    def chunk(ci, carry):
        rows = pl.ds(pl.multiple_of(ci * CHUNK, CHUNK), CHUNK)
        _gdn_chunk(rows, qkv_ref, z_ref, ba_ref, convw_ref, gp_ref, gnorm_ref, o_ref, state_ref,
                   halo_ref)
        tail = pl.ds(pl.multiple_of(ci * CHUNK + CHUNK - SUBLANES, SUBLANES), SUBLANES)
        halo_ref[...] = qkv_ref[0, tail, :]
        return carry

    lax.fori_loop(0, GDN_CHUNKS_PER_STEP, chunk, 0)


def _gdn_chunk(rows, qkv_ref, z_ref, ba_ref, convw_ref, gp_ref, gnorm_ref, o_ref, state_ref, halo_ref):
    ba = ba_ref[0, rows, :]
    beta_all = _sigmoid(ba)
    g_all = -jnp.exp(gp_ref[0:1]) * _softplus(ba + gp_ref[1:2])
    decay_all = _cumsum_rows(g_all)
    decay_t2 = jnp.transpose(jnp.concatenate([decay_all, decay_all], axis=0))

    ri = lax.broadcasted_iota(jnp.int32, (CHUNK, 2 * CHUNK), 0)
    ci = lax.broadcasted_iota(jnp.int32, (CHUNK, 2 * CHUNK), 1)
    lane_lo = ci < CHUNK
    cj = jnp.where(lane_lo, ci, ci - CHUNK)
    causal = ri >= cj
    strict = ri > cj
    lane_lo_row = lane_lo[0:1]
    gnorm = gnorm_ref[...]
    q_scale = GDN_HEAD_DIM ** -0.5
    zeros_hd = jnp.zeros((CHUNK, GDN_HEAD_DIM), F32)
    zeros_rhs = jnp.zeros((CHUNK, 2 * GDN_HEAD_DIM), F32)

    def head_inputs(h):
        lo = h * GDN_HEAD_DIM
        sl_q = slice(lo, lo + GDN_HEAD_DIM)
        sl_k = slice(GDN_WIDTH + lo, GDN_WIDTH + lo + GDN_HEAD_DIM)
        sl_v = slice(2 * GDN_WIDTH + lo, 2 * GDN_WIDTH + lo + GDN_HEAD_DIM)
        qc = _conv_silu(qkv_ref[0, rows, sl_q], halo_ref[:, sl_q], convw_ref[:, sl_q])
        kc = _conv_silu(qkv_ref[0, rows, sl_k], halo_ref[:, sl_k], convw_ref[:, sl_k])
        vc = _conv_silu(qkv_ref[0, rows, sl_v], halo_ref[:, sl_v], convw_ref[:, sl_v])
        qn = qc * (lax.rsqrt(jnp.sum(qc * qc, axis=-1, keepdims=True) + NORM_EPS) * q_scale)
        kn = kc * lax.rsqrt(jnp.sum(kc * kc, axis=-1, keepdims=True) + NORM_EPS)
        beta = beta_all[:, h:h + 1]
        dcol = decay_all[:, GDN_HEADS + h:GDN_HEADS + h + 1]
        dlast = dcol[CHUNK - 1:CHUNK]
        edec = jnp.exp(dcol)
        kb = kn * beta
        return dict(sl=sl_q, qn=qn, kn=kn, kb=kb, dcol=dcol, q_dec=qn * edec,
                    k_dec=kn * jnp.exp(dlast - dcol), sdec=jnp.exp(dlast),
                    rhs=jnp.concatenate([vc * beta, kb * edec], axis=1))

    for g0 in range(0, GDN_HEADS, GDN_HEAD_GROUP):
        pairs = [(h, h + 1) for h in range(g0, g0 + GDN_HEAD_GROUP, 2)]
        pre = [(head_inputs(h0), head_inputs(h1)) for h0, h1 in pairs]

        gamma, r = [], []
        for (h0, h1), (p0, p1) in zip(pairs, pre):
            drow = jnp.where(lane_lo_row, decay_t2[GDN_HEADS + h0:GDN_HEADS + h0 + 1],
                             decay_t2[GDN_HEADS + h1:GDN_HEADS + h1 + 1])
            dcol = jnp.where(lane_lo, p0["dcol"], p1["dcol"])
            gamma.append(jnp.exp(jnp.where(causal, dcol - drow, -jnp.inf)))
            lhs = jnp.concatenate([jnp.concatenate([p0["kb"], p1["kb"]], axis=1),
                                   jnp.concatenate([p0["qn"], p1["qn"]], axis=1)], axis=0)
            k_bd = jnp.concatenate([jnp.concatenate([p0["kn"], zeros_hd], axis=1),
                                    jnp.concatenate([zeros_hd, p1["kn"]], axis=1)], axis=0)
            r.append(_dot_nt(lhs, k_bd))
        a_low = [jnp.where(strict, ri_[:CHUNK] * gm, 0.0) for ri_, gm in zip(r, gamma)]
        qk = [ri_[CHUNK:] * gm for ri_, gm in zip(r, gamma)]

        nmat = [-a for a in a_low]
        a_parts = [_split_bf16(a) for a in a_low]
        xpow = [_pair_dot3(ap, ap, lane_lo) for ap in a_parts]
        p2 = 2
        while 2 * p2 < CHUNK:
            n_parts = [_split_bf16(nm) for nm in nmat]
            x_parts = [_split_bf16(xp) for xp in xpow]
            prod = [_pair_dot3(tuple(jnp.concatenate([n_, x_], axis=0) for n_, x_ in zip(npt, xpt)),
                               xpt, lane_lo) for npt, xpt in zip(n_parts, x_parts)]
            nmat = [nm + xp + pr[:CHUNK] for nm, xp, pr in zip(nmat, xpow, prod)]
            xpow = [pr[CHUNK:] for pr in prod]
            p2 *= 2
        prod = [_pair_dot3(_split_bf16(nm), _split_bf16(xp), lane_lo) for nm, xp in zip(nmat, xpow)]
        nmat = [nm + xp + pr for nm, xp, pr in zip(nmat, xpow, prod)]

        heads, hp, uw = [], [], []
        for (h0, h1), (p0, p1), nm in zip(pairs, pre, nmat):
            n0 = jnp.where(lane_lo, nm, 0.0)
            n1 = jnp.where(lane_lo, 0.0, nm)
            uw.append(p0["rhs"] + _dot(n0, jnp.concatenate([p0["rhs"], zeros_rhs], axis=0)))
            uw.append(p1["rhs"] + _dot(n1, jnp.concatenate([zeros_rhs, p1["rhs"]], axis=0)))
            heads += [h0, h1]
            hp += [p0, p1]
        states = [state_ref[h] for h in heads]
        ws_qs = [_dot(jnp.concatenate([uwi[:, GDN_HEAD_DIM:], p["q_dec"]], axis=0), st)
                 for uwi, p, st in zip(uw, hp, states)]
        v_new = [uwi[:, :GDN_HEAD_DIM] - wq[:CHUNK] for uwi, wq in zip(uw, ws_qs)]
        o_att = []
        for i, qki in enumerate(qk):
            v_bd = jnp.concatenate([jnp.concatenate([v_new[2 * i], zeros_hd], axis=1),
                                    jnp.concatenate([zeros_hd, v_new[2 * i + 1]], axis=1)], axis=0)
            oa = _dot(qki, v_bd)
            o_att += [oa[:, :GDN_HEAD_DIM], oa[:, GDN_HEAD_DIM:]]
        kv_upd = [_dot_tn(p["k_dec"], vn) for p, vn in zip(hp, v_new)]
        for h, p, st, wq, oa, kvu in zip(heads, hp, states, ws_qs, o_att, kv_upd):
            state_ref[h] = st * p["sdec"] + kvu
            o = wq[CHUNK:] + oa
            on = o * lax.rsqrt(jnp.mean(o * o, axis=-1, keepdims=True) + NORM_EPS) * gnorm
            zc = z_ref[0, rows, p["sl"]]
            o_ref[0, rows, p["sl"]] = (on * _silu(zc)).astype(o_ref.dtype)


def _gdn(qkvz, ba, conv_w, a_log, dt_bias, gnorm):
    b, t, _ = qkvz.shape
    gp = jnp.zeros((2, LANES), F32)
    gp = gp.at[0, GDN_HEADS:2 * GDN_HEADS].set(a_log).at[1, GDN_HEADS:2 * GDN_HEADS].set(dt_bias)
    qkv_w = 3 * GDN_WIDTH
    step_rows = GDN_CHUNKS_PER_STEP * CHUNK
    return pl.pallas_call(
        _gdn_kernel,
        grid=(b, t // step_rows),
        in_specs=[
            pl.BlockSpec((1, step_rows, qkv_w), lambda i, c: (i, c, 0)),
            pl.BlockSpec((1, step_rows, GDN_WIDTH), lambda i, c: (i, c, qkv_w // GDN_WIDTH)),
            pl.BlockSpec((1, step_rows, LANES), lambda i, c: (i, c, ba.shape[2] // LANES - 1)),
            pl.BlockSpec((CONV_K, qkv_w), lambda i, c: (0, 0)),
            pl.BlockSpec((2, LANES), lambda i, c: (0, 0)),
            pl.BlockSpec((1, GDN_HEAD_DIM), lambda i, c: (0, 0)),
        ],
        out_specs=pl.BlockSpec((1, step_rows, GDN_WIDTH), lambda i, c: (i, c, 0)),
        out_shape=jax.ShapeDtypeStruct((b, t, GDN_WIDTH), BF16),
        scratch_shapes=[pltpu.VMEM((GDN_HEADS, GDN_HEAD_DIM, GDN_HEAD_DIM), F32),
                        pltpu.VMEM((SUBLANES, qkv_w), F32)],
        compiler_params=_params(("parallel", "arbitrary")),
        name="gdn",
    )(qkvz, qkvz, ba, conv_w, gp, gnorm.reshape(1, GDN_HEAD_DIM))


def _swa_kernel(sink_ref, q_ref, kp_ref, kc_ref, vp_ref, vc_ref, o_ref):
    nb = pl.program_id(1)
    kj = lax.broadcasted_iota(jnp.int32, (WINDOW, WINDOW), 0)
    qi = lax.broadcasted_iota(jnp.int32, (WINDOW, WINDOW), 1)
    from_cur = kj <= qi
    dist = jnp.where(from_cur, qi - kj, qi - kj + WINDOW).astype(F32)
    dist = jnp.where(jnp.logical_and(nb == 0, jnp.logical_not(from_cur)), jnp.inf, dist)
    lane_lo = qi < SWA_HEAD_DIM
    q_scale = SWA_HEAD_DIM ** -0.5
    pair_w = 2 * SWA_HEAD_DIM

    def lo_hi(tile, kv):
        swapped = pltpu.roll(tile, SWA_HEAD_DIM, axis=1)
        zero = jnp.zeros_like(tile)
        if kv % 2 == 0:
            return jnp.where(lane_lo, tile, zero), jnp.where(lane_lo, zero, swapped)
        return jnp.where(lane_lo, swapped, zero), jnp.where(lane_lo, zero, tile)

    batches = []
    for kv in range(SWA_KV_HEADS):
        tile_sl = slice((kv // 2) * pair_w, (kv // 2 + 1) * pair_w)
        kc_lo, kc_hi = lo_hi(kc_ref[0, :, tile_sl], kv)
        kp_lo, kp_hi = lo_hi(kp_ref[0, :, tile_sl], kv)
        kc2 = jnp.concatenate([kc_lo, kc_hi], axis=0).astype(BF16)
        kp2 = jnp.concatenate([kp_lo, kp_hi], axis=0).astype(BF16)
        v2 = jnp.concatenate([vc_ref[0, :, tile_sl], vp_ref[0, :, tile_sl]], axis=0).astype(BF16)
        row0 = (kv % 2) * SWA_HEAD_DIM
        for pp in range(SWA_GROUP // 4):
            head0 = kv * SWA_GROUP + 4 * pp
            col = head0 * SWA_HEAD_DIM
            q2 = jnp.concatenate([q_ref[0, :, col:col + pair_w],
                                  q_ref[0, :, col + pair_w:col + 2 * pair_w]], axis=0)
            q2 = (q2 * q_scale).astype(BF16)
            s_cur = lax.dot_general(kc2, q2, (((1,), (1,)), ((), ())), preferred_element_type=F32)
            s_prev = lax.dot_general(kp2, q2, (((1,), (1,)), ((), ())), preferred_element_type=F32)
            batches.append((head0, col, v2, row0, s_cur, s_prev))

    pairs = []
    for head0, col, v2, row0, s_cur, s_prev in batches:
        for pr in range(2):
            cs = slice(pr * WINDOW, (pr + 1) * WINDOW)
            p_cur, p_prev, inv_l = [], [], []
            for half in range(2):
                head = head0 + 2 * pr + half
                rs = slice(half * WINDOW, (half + 1) * WINDOW)
                slope = 2.0 ** (-8.0 * (head + 1) / SWA_Q_HEADS)
                s = jnp.where(from_cur, s_cur[rs, cs], s_prev[rs, cs]) - slope * dist
                sink = sink_ref[0, head]
                m = jnp.maximum(jnp.max(s, axis=0, keepdims=True), sink)
                p = jnp.exp(s - m)
                inv_l.append(1.0 / (jnp.sum(p, axis=0, keepdims=True) + jnp.exp(sink - m)))
                pb = p.astype(BF16)
                zero = jnp.zeros_like(pb)
                p_cur.append(jnp.where(from_cur, pb, zero))
                p_prev.append(jnp.where(from_cur, zero, pb))
            probs = jnp.concatenate([jnp.concatenate(p_cur, axis=1),
                                     jnp.concatenate(p_prev, axis=1)], axis=0)
            pairs.append((col + pr * pair_w, v2, row0, probs, inv_l))

    outs = [lax.dot_general(v2, probs, (((0,), (0,)), ((), ())), preferred_element_type=F32)
            for _, v2, _, probs, _ in pairs]
    for (pcol, _, row0, _, inv_l), o_t in zip(pairs, outs):
        o_t = o_t[row0:row0 + SWA_HEAD_DIM]
        o_pair = jnp.concatenate([o_t[:, :WINDOW] * inv_l[0],
                                  o_t[:, WINDOW:] * inv_l[1]], axis=0)
        o_ref[0, :, pcol:pcol + pair_w] = jnp.transpose(o_pair).astype(o_ref.dtype)


def _swa(rest, sinks):
    b, t, _ = rest.shape
    kb = ROFF_KS // SWA_KV_WIDTH
    vb = ROFF_VS // SWA_KV_WIDTH
    prev = lambda n: jnp.maximum(n - 1, 0)
    return pl.pallas_call(
        _swa_kernel,
        grid=(b, t // WINDOW),
        in_specs=[
            pl.BlockSpec(memory_space=pltpu.SMEM),
            pl.BlockSpec((1, WINDOW, SWA_Q_WIDTH), lambda i, n: (i, n, ROFF_QS // SWA_Q_WIDTH)),
            pl.BlockSpec((1, WINDOW, SWA_KV_WIDTH), lambda i, n: (i, prev(n), kb)),
            pl.BlockSpec((1, WINDOW, SWA_KV_WIDTH), lambda i, n: (i, n, kb)),
            pl.BlockSpec((1, WINDOW, SWA_KV_WIDTH), lambda i, n: (i, prev(n), vb)),
            pl.BlockSpec((1, WINDOW, SWA_KV_WIDTH), lambda i, n: (i, n, vb)),
        ],
        out_specs=pl.BlockSpec((1, WINDOW, SWA_Q_WIDTH), lambda i, n: (i, n, 0)),
        out_shape=jax.ShapeDtypeStruct((b, t, SWA_Q_WIDTH), BF16),
        compiler_params=_params(("parallel", "parallel")),
        name="swa",
    )(sinks.reshape(1, SWA_Q_HEADS), rest, rest, rest, rest, rest)


def _merge_kernel(og_ref, os_ref, wg_ref, ws_ref, gg_ref, gs_ref, o_ref, wg_bf, ws_bf):
    @pl.when(pl.program_id(1) == 0)
    def _():
        for r in range(0, wg_bf.shape[0], W_CAST_ROWS):
            rows = slice(r, r + W_CAST_ROWS)
            wg_bf[rows, :] = wg_ref[rows, :].astype(BF16)
            ws_bf[rows, :] = ws_ref[rows, :].astype(BF16)

    yg = jnp.dot(og_ref[...], wg_bf[...], preferred_element_type=F32)
    ys = jnp.dot(os_ref[...], ws_bf[...], preferred_element_type=F32)
    o_ref[...] = (_sigmoid(gg_ref[...]) * yg + _sigmoid(gs_ref[...]) * ys).astype(o_ref.dtype)


def _merge(o_gdn, o_swa, w_gdn, w_swa, layer, rest2d, tm, tn=512):
    m = o_gdn.shape[0]
    gate_blk = ROFF_GATES // tn
    return pl.pallas_call(
        _merge_kernel,
        grid=(D_MODEL // tn, m // tm),
        in_specs=[
            pl.BlockSpec((tm, GDN_WIDTH), lambda j, i: (i, 0)),
            pl.BlockSpec((tm, SWA_Q_WIDTH), lambda j, i: (i, 0)),
            pl.BlockSpec((None, GDN_WIDTH, tn), lambda j, i: (layer, 0, j)),
            pl.BlockSpec((None, SWA_Q_WIDTH, tn), lambda j, i: (layer, 0, j)),
            pl.BlockSpec((tm, tn), lambda j, i: (i, gate_blk + j)),
            pl.BlockSpec((tm, tn), lambda j, i: (i, gate_blk + D_MODEL // tn + j)),
        ],
        out_specs=pl.BlockSpec((tm, tn), lambda j, i: (i, j)),
        out_shape=jax.ShapeDtypeStruct((m, D_MODEL), BF16),
        scratch_shapes=[pltpu.VMEM((GDN_WIDTH, tn), BF16), pltpu.VMEM((SWA_Q_WIDTH, tn), BF16)],
        compiler_params=_params(("parallel", "arbitrary")),
        name="merge",
    )(o_gdn, o_swa, w_gdn, w_swa, rest2d, rest2d)


def kernel(x, norm1_g, w_in, conv_w, a_log, dt_bias, gdn_norm_g, attn_sinks, w_branch_gdn,
           w_branch_swa, w_out, norm2_g, w_ff_up, w_ff_down, final_norm_g):
    b, t, d = x.shape
    depth = w_in.shape[0]
    n = b * t
    tm, tn = ROW_TILE, COL_TILE
    w_in_t = jnp.swapaxes(w_in, 1, 2)
    w_down = w_ff_down.astype(BF16)

    xf = x.reshape(n, d)
    for l in range(depth):
        h = _rmsnorm(xf, norm1_g[l], BF16)
        qkvz = _matmul_ws(h, w_in_t, l, col0=0, n_out=QKVZ_WIDTH, out_dtype=F32, tm=tm, tn=tn,
                          transposed=True, name="in_proj_qkvz")
        rest = _matmul_ws(h, w_in_t, l, col0=SRC_BA, n_out=REST_WIDTH + LANES, out_dtype=F32, tm=tm,
                          tn=tn, shift=REST_SHIFT, transposed=True, extra_col0=SRC_BA,
                          name="in_proj_rest")
        rest3 = rest.reshape(b, t, REST_WIDTH + LANES)
        o_gdn = _gdn(qkvz.reshape(b, t, QKVZ_WIDTH), rest3, conv_w[l], a_log[l],
                     dt_bias[l], gdn_norm_g[l]).reshape(n, GDN_WIDTH)
        o_swa = _swa(rest3, attn_sinks[l]).reshape(n, SWA_Q_WIDTH)
        merged = _merge(o_gdn, o_swa, w_branch_gdn, w_branch_swa, l, rest, tm=tm)
        xf, h2 = _proj_norm(merged, w_out, l, xf, norm2_g[l], tm=tm // 2)
        act = _matmul_ws(h2, w_ff_up, l, col0=0, n_out=D_FF, out_dtype=BF16, tm=tm, tn=tn,
                         epilogue="relu2", name="ff_up")
        xf = _matmul(act, w_down, l, out_dtype=F32, tm=tm, tn=tn, tk=2048, epilogue="residual",
                     res=xf, name="ff_down")
    return _rmsnorm(xf, final_norm_g, F32).reshape(b, t, d)
```
